```python
import jax, jax.numpy as jnp
from jax import lax
import numpy as np

D_MODEL = 1024
BATCH = 8
SEQ = 2048
DEPTH = 1

MIX_WIDTH = D_MODEL
HEAD_DIM = 64
RWKV_WIDTH = MIX_WIDTH // 2
ATTN_WIDTH = MIX_WIDTH - RWKV_WIDTH
RWKV_HEADS = RWKV_WIDTH // HEAD_DIM
ATTN_HEADS = ATTN_WIDTH // HEAD_DIM
DECAY_LORA = 64
AAA_LORA = 64
GATE_LORA = 128
DILATED_PAIRS = ((128, 1), (512, 4), (2048, 16))
ATTN_BLOCK = 128
D_FF = -(-8 * D_MODEL // (3 * 256)) * 256
NORM_EPS = 1e-6
GN_EPS = 64e-5
RWKV_SHIFT_COLS = 3 * RWKV_WIDTH + DECAY_LORA + AAA_LORA + GATE_LORA
IN_COLS = RWKV_SHIFT_COLS + 3 * ATTN_WIDTH

kernel_name = 'hybrid_rwkv7_dilated_attn'


def rmsnorm(x, g):
    xf = x.astype(jnp.float32)
    y = xf * lax.rsqrt(jnp.mean(xf * xf, axis=-1, keepdims=True) + NORM_EPS)
    return (y * g.astype(jnp.float32)).astype(x.dtype)


def wkv7_scan(r, w, k, v, kk, a):
    B, S, H, N = r.shape

    def step(state, inp):
        r_t, w_t, k_t, v_t, kk_t, a_t = inp
        sa = jnp.einsum('bhij,bhj->bhi', state, -kk_t)
        state = (state * w_t[:, :, None, :]
                 + sa[..., :, None] * (kk_t * a_t)[..., None, :]
                 + v_t[..., :, None] * k_t[..., None, :])
        y = jnp.einsum('bhij,bhj->bhi', state, r_t)
        return state, y

    xs = tuple(t.transpose(1, 0, 2, 3) for t in (r, w, k, v, kk, a))
    init = jnp.zeros((B, H, N, N), jnp.float32)
    _, ys = lax.scan(step, init, xs)
    return ys.transpose(1, 0, 2, 3)


def rwkv7_mixer(p, mu, w0, w2, a0, a2, g2, k_k, k_a, r_k, ln_w, ln_b):
    B, S, _ = p.shape
    prev = jnp.pad(p[:, :-1], ((0, 0), (1, 0), (0, 0)))
    p = p + (prev - p) * mu
    W = RWKV_WIDTH
    r, k, v, xw, xa, xg = jnp.split(
        p, [W, 2 * W, 3 * W, 3 * W + DECAY_LORA, 3 * W + DECAY_LORA + AAA_LORA], axis=-1)
    w = -jax.nn.softplus(-(w0 + jnp.tanh(xw) @ w2)) - 0.5
    decay = jnp.exp(-jnp.exp(w.astype(jnp.float32)))
    a = jax.nn.sigmoid(a0 + xa @ a2)
    g = jax.nn.sigmoid(xg) @ g2

    def hd(t):
        return t.reshape(B, S, RWKV_HEADS, HEAD_DIM).astype(jnp.float32)

    kk = hd(k * k_k)
    kk = kk / jnp.maximum(jnp.sqrt(jnp.sum(kk * kk, axis=-1, keepdims=True)), 1e-12)
    k = k * (1 + (a - 1) * k_a)
    rh, kh, vh, ah, wh = hd(r), hd(k), hd(v), hd(a), hd(decay)
    y = wkv7_scan(rh, wh, kh, vh, kk, ah)
    mean = jnp.mean(y, axis=-1, keepdims=True)
    var = jnp.mean(jnp.square(y - mean), axis=-1, keepdims=True)
    y = ((y - mean) * lax.rsqrt(var + GN_EPS)).reshape(B, S, W)
    y = y * ln_w.astype(jnp.float32) + ln_b.astype(jnp.float32)
    bonus = jnp.sum(rh * kh * r_k.astype(jnp.float32), axis=-1, keepdims=True) * vh
    y = y + bonus.reshape(B, S, W)
    return (y * g.astype(jnp.float32)).astype(p.dtype)


def dilated_branch(q, k, v, window, dilation):
    B, H, S, Dh = q.shape
    L = S // dilation
    span = window // dilation
    nb = -(-L // ATTN_BLOCK)
    Lp = nb * ATTN_BLOCK

    def to_sub(t):
        t = t.reshape(B, H, L, dilation, Dh).transpose(0, 1, 3, 2, 4)
        t = jnp.pad(t, ((0, 0), (0, 0), (0, 0), (0, Lp - L), (0, 0)))
        return t.reshape(B, H, dilation, nb, ATTN_BLOCK, Dh)

    def with_prev(t):
        prv = jnp.pad(t[:, :, :, :-1], ((0, 0), (0, 0), (0, 0), (1, 0), (0, 0), (0, 0)))
        return jnp.concatenate([prv, t], axis=4)

    qb = to_sub(q)
    kc = with_prev(to_sub(k))
    vc = with_prev(to_sub(v))
    s = jnp.einsum('bhrnqe,bhrnke->bhrnqk', qb, kc) * (Dh ** -0.5)
    qi = jnp.arange(ATTN_BLOCK)[:, None] + ATTN_BLOCK
    kj = jnp.arange(2 * ATTN_BLOCK)[None, :]
    rel = qi - kj
    blk = jnp.arange(nb)[:, None, None]
    valid = (rel >= 0) & (rel <= span) & ((blk - 1) * ATTN_BLOCK + kj >= 0)
    s = jnp.where(valid, s, -jnp.inf)
    m = jnp.max(s, axis=-1, keepdims=True)
    pe = jnp.exp(s - m)
    den = jnp.sum(pe, axis=-1, keepdims=True)
    o = jnp.einsum('bhrnqk,bhrnke->bhrnqe', pe, vc) / den
    lse = (m + jnp.log(den))[..., 0]
    o = o.reshape(B, H, dilation, Lp, Dh)[:, :, :, :L].transpose(0, 1, 3, 2, 4).reshape(B, H, S, Dh)
    lse = lse.reshape(B, H, dilation, Lp)[..., :L].transpose(0, 1, 3, 2).reshape(B, H, S)
    return o, lse


def dilated_attention(q, k, v, out_g):
    B, S, _ = q.shape

    def heads(t):
        return t.reshape(B, S, ATTN_HEADS, HEAD_DIM).transpose(0, 2, 1, 3).astype(jnp.float32)

    qh, kh, vh = heads(q), heads(k), heads(v)
    outs, lses = [], []
    for window, dilation in DILATED_PAIRS:
        o, l = dilated_branch(qh, kh, vh, window, dilation)
        outs.append(o)
        lses.append(l)
    alpha = jax.nn.softmax(jnp.stack(lses), axis=0)
    o = jnp.sum(alpha[..., None] * jnp.stack(outs), axis=0).transpose(0, 2, 1, 3)
    o = o * lax.rsqrt(jnp.mean(o * o, axis=-1, keepdims=True) + NORM_EPS)
    o = o.reshape(B, S, ATTN_WIDTH) * out_g.astype(jnp.float32)
    return o.astype(q.dtype)


def setup_inputs(seed: int = 0) -> dict:
    key = jax.random.key(seed)
    ks = jax.random.split(key, 24)
    nrm = jax.random.normal
    f32 = jnp.float32
    return {
        'x': nrm(ks[0], (BATCH, SEQ, D_MODEL), f32),
        'mix_norm_g': 1.0 + 0.02 * nrm(ks[1], (DEPTH, D_MODEL), f32),
        'w_in': nrm(ks[2], (DEPTH, D_MODEL, IN_COLS), f32) * D_MODEL ** -0.5,
        'mu_shift': jax.random.uniform(ks[3], (DEPTH, RWKV_SHIFT_COLS), f32),
        'decay_w0': jax.random.uniform(ks[4], (DEPTH, RWKV_WIDTH), f32, minval=-6.0, maxval=-1.0),
        'decay_w2': nrm(ks[5], (DEPTH, DECAY_LORA, RWKV_WIDTH), f32) * 0.1 * DECAY_LORA ** -0.5,
        'iclr_a0': 0.1 * nrm(ks[6], (DEPTH, RWKV_WIDTH), f32),
        'iclr_a2': nrm(ks[7], (DEPTH, AAA_LORA, RWKV_WIDTH), f32) * 0.1 * AAA_LORA ** -0.5,
        'gate_g2': nrm(ks[8], (DEPTH, GATE_LORA, RWKV_WIDTH), f32) * GATE_LORA ** -0.5,
        'k_k': 0.85 + 0.02 * nrm(ks[9], (DEPTH, RWKV_WIDTH), f32),
        'k_a': 1.0 + 0.02 * nrm(ks[10], (DEPTH, RWKV_WIDTH), f32),
        'r_k': 0.1 * nrm(ks[11], (DEPTH, RWKV_HEADS, HEAD_DIM), f32),
        'ln_x_w': 1.0 + 0.02 * nrm(ks[12], (DEPTH, RWKV_WIDTH), f32),
        'ln_x_b': 0.02 * nrm(ks[13], (DEPTH, RWKV_WIDTH), f32),
        'attn_out_g': 1.0 + 0.02 * nrm(ks[14], (DEPTH, ATTN_WIDTH), f32),
        'w_out': nrm(ks[15], (DEPTH, MIX_WIDTH, D_MODEL), f32) * MIX_WIDTH ** -0.5,
        'ffn_norm_g': 1.0 + 0.02 * nrm(ks[16], (DEPTH, D_MODEL), f32),
        'w_gate': nrm(ks[17], (DEPTH, D_MODEL, D_FF), f32) * D_MODEL ** -0.5,
        'w_up': nrm(ks[18], (DEPTH, D_MODEL, D_FF), f32) * D_MODEL ** -0.5,
        'w_down': nrm(ks[19], (DEPTH, D_FF, D_MODEL), f32) * D_FF ** -0.5,
        'final_norm_g': 1.0 + 0.02 * nrm(ks[20], (D_MODEL,), f32),
    }


def reference(x, mix_norm_g, w_in, mu_shift, decay_w0, decay_w2, iclr_a0, iclr_a2, gate_g2,
              k_k, k_a, r_k, ln_x_w, ln_x_b, attn_out_g, w_out, ffn_norm_g, w_gate, w_up,
              w_down, final_norm_g):
    c0 = RWKV_SHIFT_COLS
    for i in range(DEPTH):
        h = rmsnorm(x, mix_norm_g[i])
        proj = h @ w_in[i]
        p_a, q, k, v = jnp.split(proj, [c0, c0 + ATTN_WIDTH, c0 + 2 * ATTN_WIDTH], axis=-1)
        y_a = rwkv7_mixer(p_a, mu_shift[i], decay_w0[i], decay_w2[i], iclr_a0[i], iclr_a2[i],
                          gate_g2[i], k_k[i], k_a[i], r_k[i], ln_x_w[i], ln_x_b[i])
        y_b = dilated_attention(q, k, v, attn_out_g[i])
        x = x + jnp.concatenate([y_a, y_b.astype(y_a.dtype)], axis=-1) @ w_out[i]
        h = rmsnorm(x, ffn_norm_g[i])
        x = x + (jax.nn.silu(h @ w_gate[i]) * (h @ w_up[i])) @ w_down[i]
    return rmsnorm(x, final_norm_g)
```

```python
import functools

import jax
import jax.numpy as jnp
from jax import lax
from jax.experimental import pallas as pl
from jax.experimental.pallas import tpu as pltpu

HEAD_DIM = 64
LANES = 128
DECAY_LORA = 64
AAA_LORA = 64
GATE_LORA = 128
DILATED_PAIRS = ((128, 1), (512, 4), (2048, 16))
ATTN_BLOCK = 128
NORM_EPS = 1e-6
GN_EPS = 64e-5
CHUNK = 64
VMEM_LIMIT_BYTES = 56 * 1024 * 1024

F32 = jnp.float32
BF16 = jnp.bfloat16


def _dot(a, b, dims=((1,), (0,)), exact=False):
    dn = (dims, ((), ()))
    if exact:
        return lax.dot_general(a, b, dn, precision=lax.Precision.HIGHEST,
                               preferred_element_type=F32)
    return lax.dot_general(a.astype(BF16), b.astype(BF16), dn, preferred_element_type=F32)


_NT = ((1,), (1,))
_TN = ((0,), (0,))


def _rmsnorm(x, g):
    return x * lax.rsqrt(jnp.mean(x * x, axis=-1, keepdims=True) + NORM_EPS) * g


def _sigmoid(x):
    return 1.0 / (1.0 + jnp.exp(-x))


def _inproj_kernel(x_ref, g_ref, w_ref, o_ref):
    h = _rmsnorm(x_ref[...], g_ref[...])
    o_ref[...] = jnp.dot(h.astype(BF16), w_ref[...], preferred_element_type=F32)


def _inproj(x2d, g, w_bf16, tm):
    t, d = x2d.shape
    n = w_bf16.shape[1]
    return pl.pallas_call(
        _inproj_kernel,
        grid=(t // tm,),
        in_specs=[
            pl.BlockSpec((tm, d), lambda i: (i, 0)),
            pl.BlockSpec((1, d), lambda i: (0, 0)),
            pl.BlockSpec((d, n), lambda i: (0, 0)),
        ],
        out_specs=pl.BlockSpec((tm, n), lambda i: (i, 0)),
        out_shape=jax.ShapeDtypeStruct((t, n), F32),
        compiler_params=pltpu.CompilerParams(
            dimension_semantics=("arbitrary",), vmem_limit_bytes=VMEM_LIMIT_BYTES),
        name="inproj",
    )(x2d, g, w_bf16)


def _unit_lower_inverse(n_mat):
    c = n_mat.shape[0]
    ri = lax.broadcasted_iota(jnp.int32, (c, c), 0)
    ci = lax.broadcasted_iota(jnp.int32, (c, c), 1)
    same16 = (ri >> 4) == (ci >> 4)
    same32 = (ri >> 5) == (ci >> 5)
    eye = jnp.where(ri == ci, 1.0, 0.0).astype(F32)
    nd = jnp.where(same16, n_mat, 0.0)
    t = eye + nd
    p = nd
    for _ in range(3):
        p = _dot(p, p, exact=True)
        t = t + _dot(t, p, exact=True)
    n32 = jnp.where(same32 & jnp.logical_not(same16), n_mat, 0.0)
    t = t + _dot(_dot(t, n32, exact=True), t, exact=True)
    n64 = jnp.where(same32, 0.0, n_mat)
    t = t + _dot(_dot(t, n64, exact=True), t, exact=True)
    return t


def _rwkv_kernel(p_ref, mu_ref, w0_ref, w2_ref, a0_ref, a2_ref, g2_ref, kk_ref, ka_ref, rk_ref,
                 lnw_ref, lnb_ref, o_ref, carry_ref, state_ref):
    c = CHUNK
    w = w0_ref.shape[-1]
    n_pairs = w // LANES

    @pl.when(pl.program_id(1) == 0)
    def _():
        carry_ref[...] = jnp.zeros_like(carry_ref)
        state_ref[...] = jnp.zeros_like(state_ref)

    p = p_ref[...]
    row = lax.broadcasted_iota(jnp.int32, p.shape, 0)
    prev = jnp.where(row == 0, carry_ref[...], pltpu.roll(p, 1, axis=0))
    carry_ref[...] = p[c - 1:c, :]
    p = p + (prev - p) * mu_ref[...]

    r = p[:, 0:w]
    k = p[:, w:2 * w]
    v = p[:, 2 * w:3 * w]
    xwa = p[:, 3 * w:3 * w + LANES]
    xg = p[:, 3 * w + LANES:3 * w + 2 * LANES]

    z = w0_ref[...] + _dot(jnp.tanh(xwa), w2_ref[...])
    wlog = -(jnp.maximum(-z, 0.0) + jnp.log(1.0 + jnp.exp(-jnp.abs(z)))) - 0.5
    ld = -jnp.exp(wlog)
    a = _sigmoid(a0_ref[...] + _dot(xwa, a2_ref[...]))
    g = _dot(_sigmoid(xg), g2_ref[...])

    lane = lax.broadcasted_iota(jnp.int32, (c, LANES), 1)
    head0 = lane < HEAD_DIM

    def head_sum(x):
        s0 = jnp.sum(jnp.where(head0, x, 0.0), axis=-1, keepdims=True)
        s1 = jnp.sum(jnp.where(head0, 0.0, x), axis=-1, keepdims=True)
        return jnp.where(head0, s0, s1)

    ti = lax.broadcasted_iota(jnp.int32, (c, c), 0)
    si = lax.broadcasted_iota(jnp.int32, (c, c), 1)
    strict = si < ti
    incl = si <= ti
    tri = jnp.where(incl, 1.0, 0.0).astype(F32)
    l_incl = _dot(tri, ld, exact=True)
    l_excl = l_incl - ld
    e_incl = jnp.exp(l_incl)
    e_excl = jnp.exp(l_excl)
    e_inv = jnp.exp(-l_incl)
    w_chunk = e_incl[c - 1:c, :]

    kk_all = k * kk_ref[...]
    k2_all = k * (1.0 + (a - 1.0) * ka_ref[...])
    rk_all = r * k2_all * rk_ref[...]

    t2 = lax.broadcasted_iota(jnp.int32, (c, 2 * c), 0)
    s2 = lax.broadcasted_iota(jnp.int32, (c, 2 * c), 1) & (c - 1)
    incl2 = s2 <= t2
    bd_r = lax.broadcasted_iota(jnp.int32, (LANES, LANES), 0) < HEAD_DIM
    bd_c = lax.broadcasted_iota(jnp.int32, (LANES, LANES), 1) < HEAD_DIM
    blockdiag = bd_r == bd_c

    for j in range(n_pairs):
        sl = slice(j * LANES, (j + 1) * LANES)
        kk = kk_all[:, sl]
        nrm = jnp.sqrt(head_sum(kk * kk))
        kk = kk / jnp.maximum(nrm, 1e-12)
        a_j = a[:, sl]
        v_j = v[:, sl]
        at = -kk * e_excl[:, sl]
        rt = r[:, sl] * e_incl[:, sl]
        bt = kk * a_j * e_inv[:, sl]
        kt = k2_all[:, sl] * e_inv[:, sl]
        bk = jnp.concatenate([bt, kt], axis=0)
        hs = state_ref[j]

        t_inv, a_ak, m_r = [], [], []
        for hmask in (head0, jnp.logical_not(head0)):
            at_h = jnp.where(hmask, at, 0.0)
            rt_h = jnp.where(hmask, rt, 0.0)
            n_ab = jnp.where(strict, _dot(at_h, bt, _NT, exact=True), 0.0)
            a_ak.append(jnp.where(strict, _dot(at_h, kt, _NT, exact=True), 0.0))
            m_r.append(jnp.where(incl2, _dot(rt_h, bk, _NT, exact=True), 0.0))
            t_inv.append(_unit_lower_inverse(n_ab))

        arh = _dot(jnp.concatenate([at, rt], axis=0), hs, _NT, exact=True)
        x = arh[:c] + jnp.where(head0, _dot(a_ak[0], v_j, exact=True), _dot(a_ak[1], v_j, exact=True))
        u = jnp.where(head0, _dot(t_inv[0], x, exact=True), _dot(t_inv[1], x, exact=True))
        uv = jnp.concatenate([u, v_j], axis=0)
        y = arh[c:] + jnp.where(head0, _dot(m_r[0], uv, exact=True), _dot(m_r[1], uv, exact=True))
        wc = w_chunk[:, sl]
        hs_new = hs * wc + _dot(uv, bk * wc, _TN, exact=True)
        state_ref[j] = jnp.where(blockdiag, hs_new, 0.0)

        mean = head_sum(y) * (1.0 / HEAD_DIM)
        yc = y - mean
        var = head_sum(yc * yc) * (1.0 / HEAD_DIM)
        yn = yc * lax.rsqrt(var + GN_EPS) * lnw_ref[:, sl] + lnb_ref[:, sl]
        bonus = head_sum(rk_all[:, sl]) * v_j
        o_ref[:, sl] = (yn + bonus) * g[:, sl]


def _rwkv(proj, mu, w0, w2p, a0, a2p, g2, k_k, k_a, r_k, ln_w, ln_b):
    b, s, _ = proj.shape
    w = w0.shape[-1]
    shift_cols = mu.shape[-1]
    c = CHUNK
    row = lambda n: pl.BlockSpec((1, n), lambda bi, ci: (0, 0))
    mat = lambda m, n: pl.BlockSpec((m, n), lambda bi, ci: (0, 0))
    return pl.pallas_call(
        _rwkv_kernel,
        grid=(b, s // c),
        in_specs=[
            pl.BlockSpec((None, c, shift_cols), lambda bi, ci: (bi, ci, 0)),
            row(shift_cols), row(w), mat(LANES, w), row(w), mat(LANES, w), mat(GATE_LORA, w),
            row(w), row(w), row(w), row(w), row(w),
        ],
        out_specs=pl.BlockSpec((None, c, w), lambda bi, ci: (bi, ci, 0)),
        out_shape=jax.ShapeDtypeStruct((b, s, w), F32),
        scratch_shapes=[
            pltpu.VMEM((1, shift_cols), F32),
            pltpu.VMEM((w // LANES, LANES, LANES), F32),
        ],
        compiler_params=pltpu.CompilerParams(
            dimension_semantics=("arbitrary", "arbitrary"), vmem_limit_bytes=VMEM_LIMIT_BYTES),
        name="rwkv7",
    )(proj, mu, w0, w2p, a0, a2p, g2, k_k, k_a, r_k, ln_w, ln_b)


def _attn_kernel(q_ref, k_ref, v_ref, g_ref, o_ref, ob_ref, lse_ref):
    s_len = q_ref.shape[0]
    blk = ATTN_BLOCK
    scale = HEAD_DIM ** -0.5
    lane = lax.broadcasted_iota(jnp.int32, (blk, LANES), 1)
    head0 = lane < HEAD_DIM

    def block(bi, dil, span, start, prev_start, has_prev_static, prev_valid):
        rows = pl.ds(start, blk, stride=dil) if dil > 1 else pl.ds(start, blk)
        q = q_ref[rows, :] * scale
        kc = k_ref[rows, :]
        vc = v_ref[rows, :]
        if has_prev_static:
            prows = pl.ds(prev_start, blk, stride=dil) if dil > 1 else pl.ds(prev_start, blk)
            kcat = jnp.concatenate([k_ref[prows, :], kc], axis=0)
            vcat = jnp.concatenate([v_ref[prows, :], vc], axis=0)
            nk = 2 * blk
            qi = lax.broadcasted_iota(jnp.int32, (blk, nk), 0) + blk
            kj = lax.broadcasted_iota(jnp.int32, (blk, nk), 1)
            rel = qi - kj
            valid = (rel >= 0) & (rel <= span) & (prev_valid * blk + kj >= blk)
        else:
            kcat, vcat, nk = kc, vc, blk
            qi = lax.broadcasted_iota(jnp.int32, (blk, nk), 0)
            kj = lax.broadcasted_iota(jnp.int32, (blk, nk), 1)
            rel = qi - kj
            valid = (rel >= 0) & (rel <= span)
        outs, lses = [], []
        for hmask in (head0, jnp.logical_not(head0)):
            qh = jnp.where(hmask, q, 0.0)
            sc = _dot(qh, kcat, _NT)
            sc = jnp.where(valid, sc, -jnp.inf)
            m = jnp.max(sc, axis=-1, keepdims=True)
            pe = jnp.exp(sc - m)
            den = jnp.sum(pe, axis=-1, keepdims=True)
            outs.append(_dot(pe, vcat) / den)
            lses.append(m + jnp.log(den))
        ob_ref[bi, rows, :] = jnp.where(head0, outs[0], outs[1])
        lse_ref[bi, rows, :] = jnp.where(head0, lses[0], lses[1])

    for bi, (window, dil) in enumerate(DILATED_PAIRS):
        sub_len = s_len // dil
        span = window // dil
        nb = -(-sub_len // blk)
        assert sub_len % blk == 0
        if nb == 1:
            def body(r, carry, bi=bi, dil=dil, span=span):
                block(bi, dil, span, r, 0, False, None)
                return carry
            lax.fori_loop(0, dil, body, 0)
        else:
            def body(i, carry, bi=bi, dil=dil, span=span, nb=nb):
                assert nb & (nb - 1) == 0
                r = lax.shift_right_logical(i, nb.bit_length() - 1)
                n = i & (nb - 1)
                start = n * (blk * dil) + r
                prev_start = jnp.maximum(n - 1, 0) * (blk * dil) + r
                block(bi, dil, span, start, prev_start, True, jnp.minimum(n, 1))
                return carry
            lax.fori_loop(0, dil * nb, body, 0)

    tile = 256
    lane_t = lax.broadcasted_iota(jnp.int32, (tile, LANES), 1)
    head0_t = lane_t < HEAD_DIM

    def mix(i, carry):
        rows = pl.ds(pl.multiple_of(i * tile, tile), tile)
        l0, l1, l2 = lse_ref[0, rows, :], lse_ref[1, rows, :], lse_ref[2, rows, :]
        mx = jnp.maximum(jnp.maximum(l0, l1), l2)
        e0, e1, e2 = jnp.exp(l0 - mx), jnp.exp(l1 - mx), jnp.exp(l2 - mx)
        den = e0 + e1 + e2
        o = (e0 / den) * ob_ref[0, rows, :] + (e1 / den) * ob_ref[1, rows, :] + (e2 / den) * ob_ref[2, rows, :]
        o2 = o * o
        s0 = jnp.sum(jnp.where(head0_t, o2, 0.0), axis=-1, keepdims=True)
        s1 = jnp.sum(jnp.where(head0_t, 0.0, o2), axis=-1, keepdims=True)
        ms = jnp.where(head0_t, s0, s1) * (1.0 / HEAD_DIM)
        o_ref[rows, :] = o * lax.rsqrt(ms + NORM_EPS) * g_ref[...]
        return carry

    lax.fori_loop(0, s_len // tile, mix, 0)


def _attention(proj, out_g, q_col0, width):
    b, s, _ = proj.shape
    n_pairs = width // LANES
    qb = q_col0 // LANES
    spec = lambda off: pl.BlockSpec((None, s, LANES), lambda bi, pi: (bi, 0, off + pi))
    return pl.pallas_call(
        _attn_kernel,
        grid=(b, n_pairs),
        in_specs=[spec(qb), spec(qb + n_pairs), spec(qb + 2 * n_pairs),
                  pl.BlockSpec((1, LANES), lambda bi, pi: (0, pi))],
        out_specs=pl.BlockSpec((None, s, LANES), lambda bi, pi: (bi, 0, pi)),
        out_shape=jax.ShapeDtypeStruct((b, s, width), F32),
        scratch_shapes=[
            pltpu.VMEM((len(DILATED_PAIRS), s, LANES), F32),
            pltpu.VMEM((len(DILATED_PAIRS), s, LANES), F32),
        ],
        compiler_params=pltpu.CompilerParams(
            dimension_semantics=("arbitrary", "arbitrary"), vmem_limit_bytes=VMEM_LIMIT_BYTES),
        name="dilated_attn",
    )(proj, proj, proj, out_g)


def _ffn_kernel(x_ref, ya_ref, yb_ref, wo_ref, gn_ref, wg_ref, wu_ref, wd_ref, gf_ref, o_ref, *, final):
    wa = ya_ref.shape[-1]
    x1 = (x_ref[...]
          + jnp.dot(ya_ref[...].astype(BF16), wo_ref[0:wa, :], preferred_element_type=F32)
          + jnp.dot(yb_ref[...].astype(BF16), wo_ref[wa:, :], preferred_element_type=F32))
    h = _rmsnorm(x1, gn_ref[...]).astype(BF16)
    gate = jnp.dot(h, wg_ref[...], preferred_element_type=F32)
    up = jnp.dot(h, wu_ref[...], preferred_element_type=F32)
    act = (gate * _sigmoid(gate)) * up
    x2 = x1 + jnp.dot(act.astype(BF16), wd_ref[...], preferred_element_type=F32)
    o_ref[...] = _rmsnorm(x2, gf_ref[...]) if final else x2


def _ffn(x2d, ya, yb, wo, gn, wg, wu, wd, gf, tm, final):
    t, d = x2d.shape
    wa, wb = ya.shape[-1], yb.shape[-1]
    dff = wg.shape[-1]
    const = lambda m, n: pl.BlockSpec((m, n), lambda i: (0, 0), pipeline_mode=pl.Buffered(1))
    return pl.pallas_call(
        functools.partial(_ffn_kernel, final=final),
        grid=(t // tm,),
        in_specs=[
            pl.BlockSpec((tm, d), lambda i: (i, 0)),
            pl.BlockSpec((tm, wa), lambda i: (i, 0)),
            pl.BlockSpec((tm, wb), lambda i: (i, 0)),
            const(wa + wb, d), const(1, d), const(d, dff), const(d, dff), const(dff, d), const(1, d),
        ],
        out_specs=pl.BlockSpec((tm, d), lambda i: (i, 0)),
        out_shape=jax.ShapeDtypeStruct((t, d), F32),
        compiler_params=pltpu.CompilerParams(
            dimension_semantics=("arbitrary",), vmem_limit_bytes=VMEM_LIMIT_BYTES),
        name="outproj_ffn",
    )(x2d, ya, yb, wo, gn, wg, wu, wd, gf)


def kernel(x, mix_norm_g, w_in, mu_shift, decay_w0, decay_w2, iclr_a0, iclr_a2, gate_g2, k_k, k_a, r_k,
           ln_x_w, ln_x_b, attn_out_g, w_out, ffn_norm_g, w_gate, w_up, w_down, final_norm_g):
    b, s, d = x.shape
    depth = w_in.shape[0]
    w = decay_w0.shape[-1]
    shift_cols = mu_shift.shape[-1]
    attn_w = attn_out_g.shape[-1]
    assert shift_cols == 3 * w + DECAY_LORA + AAA_LORA + GATE_LORA
    assert DECAY_LORA + AAA_LORA == LANES and shift_cols % LANES == 0
    tm = 256
    x2d = x.reshape(b * s, d)
    for i in range(depth):
        proj = _inproj(x2d, mix_norm_g[i][None], w_in[i].astype(BF16), tm).reshape(b, s, -1)
        zeros = jnp.zeros((LANES - DECAY_LORA, w), F32)
        w2p = jnp.concatenate([decay_w2[i], zeros], axis=0)
        a2p = jnp.concatenate([zeros, iclr_a2[i]], axis=0)
        y_a = _rwkv(proj, mu_shift[i][None], decay_w0[i][None], w2p, iclr_a0[i][None], a2p, gate_g2[i],
                    k_k[i][None], k_a[i][None], r_k[i].reshape(1, w), ln_x_w[i][None], ln_x_b[i][None])
        y_b = _attention(proj, attn_out_g[i][None], shift_cols, attn_w)
        x2d = _ffn(x2d, y_a.reshape(b * s, w), y_b.reshape(b * s, attn_w), w_out[i].astype(BF16),
                   ffn_norm_g[i][None], w_gate[i].astype(BF16), w_up[i].astype(BF16),
                   w_down[i].astype(BF16), final_norm_g[None], tm, final=(i == depth - 1))
    return x2d.reshape(b, s, d)
```

```python
import functools

import jax
import jax.numpy as jnp
from jax import lax
from jax.experimental import pallas as pl
from jax.experimental.pallas import tpu as pltpu

HEAD_DIM = 64
LANES = 128
DECAY_LORA = 64
AAA_LORA = 64
GATE_LORA = 128
DILATED_PAIRS = ((128, 1), (512, 4), (2048, 16))
ATTN_BLOCK = 128
NORM_EPS = 1e-6
GN_EPS = 64e-5
CHUNK = 64
VMEM_LIMIT_BYTES = 56 * 1024 * 1024

F32 = jnp.float32
BF16 = jnp.bfloat16


def _dot(a, b, dims=((1,), (0,))):
    return lax.dot_general(a.astype(BF16), b.astype(BF16), (dims, ((), ())),
                           preferred_element_type=F32)


_NT = ((1,), (1,))
_TN = ((0,), (0,))


def _rmsnorm(x, g):
    return x * lax.rsqrt(jnp.mean(x * x, axis=-1, keepdims=True) + NORM_EPS) * g


def _sigmoid(x):
    return 1.0 / (1.0 + jnp.exp(-x))


def _inproj_kernel(x_ref, g_ref, w_ref, o_ref):
    h = _rmsnorm(x_ref[...], g_ref[...])
    o_ref[...] = jnp.dot(h.astype(BF16), w_ref[...], preferred_element_type=F32)


def _inproj(x2d, g, w_bf16, tm):
    t, d = x2d.shape
    n = w_bf16.shape[1]
    return pl.pallas_call(
        _inproj_kernel,
        grid=(t // tm,),
        in_specs=[
            pl.BlockSpec((tm, d), lambda i: (i, 0)),
            pl.BlockSpec((1, d), lambda i: (0, 0)),
            pl.BlockSpec((d, n), lambda i: (0, 0)),
        ],
        out_specs=pl.BlockSpec((tm, n), lambda i: (i, 0)),
        out_shape=jax.ShapeDtypeStruct((t, n), F32),
        compiler_params=pltpu.CompilerParams(
            dimension_semantics=("arbitrary",), vmem_limit_bytes=VMEM_LIMIT_BYTES),
        name="inproj",
    )(x2d, g, w_bf16)


def _unit_lower_inverses(n_mats):
    c = n_mats[0].shape[0]
    ri = lax.broadcasted_iota(jnp.int32, (c, c), 0)
    ci = lax.broadcasted_iota(jnp.int32, (c, c), 1)
    same16 = (ri >> 4) == (ci >> 4)
    same32 = (ri >> 5) == (ci >> 5)
    eye = jnp.where(ri == ci, 1.0, 0.0).astype(F32)
    p = [jnp.where(same16, n, 0.0) for n in n_mats]
    t = [eye + n for n in p]
    for _ in range(3):
        p = [_dot(x, x) for x in p]
        t = [x + _dot(x, y) for x, y in zip(t, p)]
    for off in ([jnp.where(same32 & jnp.logical_not(same16), n, 0.0) for n in n_mats],
                [jnp.where(same32, 0.0, n) for n in n_mats]):
        q = [_dot(x, n) for x, n in zip(t, off)]
        t = [x + _dot(y, x) for x, y in zip(t, q)]
    return t


def _rwkv_kernel(p_ref, mu_ref, w0_ref, w2_ref, a0_ref, a2_ref, g2_ref, kk_ref, ka_ref, rk_ref,
                 lnw_ref, lnb_ref, o_ref, carry_ref, state_ref):
    c = CHUNK
    w = w0_ref.shape[-1]
    n_pairs = w // LANES

    @pl.when(pl.program_id(1) == 0)
    def _():
        carry_ref[...] = jnp.zeros_like(carry_ref)
        state_ref[...] = jnp.zeros_like(state_ref)

    p = p_ref[...]
    row = lax.broadcasted_iota(jnp.int32, p.shape, 0)
    prev = jnp.where(row == 0, carry_ref[...], pltpu.roll(p, 1, axis=0))
    carry_ref[...] = p[c - 1:c, :]
    p = p + (prev - p) * mu_ref[...]

    r = p[:, 0:w]
    k = p[:, w:2 * w]
    v = p[:, 2 * w:3 * w]
    xwa = p[:, 3 * w:3 * w + LANES]
    xg = p[:, 3 * w + LANES:3 * w + 2 * LANES]

    z = w0_ref[...] + _dot(jnp.tanh(xwa), w2_ref[...])
    wlog = -(jnp.maximum(-z, 0.0) + jnp.log(1.0 + jnp.exp(-jnp.abs(z)))) - 0.5
    ld = -jnp.exp(wlog)
    a = _sigmoid(a0_ref[...] + _dot(xwa, a2_ref[...]))
    g = _dot(_sigmoid(xg), g2_ref[...])

    lane = lax.broadcasted_iota(jnp.int32, (c, LANES), 1)
    head0 = lane < HEAD_DIM

    def head_sum(x):
        s0 = jnp.sum(jnp.where(head0, x, 0.0), axis=-1, keepdims=True)
        s1 = jnp.sum(jnp.where(head0, 0.0, x), axis=-1, keepdims=True)
        return jnp.where(head0, s0, s1)

    ti = lax.broadcasted_iota(jnp.int32, (c, c), 0)
    si = lax.broadcasted_iota(jnp.int32, (c, c), 1)
    tri = jnp.where(si <= ti, 1.0, 0.0).astype(BF16)
    ld_hi = ld.astype(BF16)
    ld_r1 = ld - ld_hi.astype(F32)
    ld_mid = ld_r1.astype(BF16)
    ld_lo = (ld_r1 - ld_mid.astype(F32)).astype(BF16)
    l_incl = _dot(tri, ld_hi) + _dot(tri, ld_mid) + _dot(tri, ld_lo)
    l_excl = l_incl - ld
    e_incl = jnp.exp(l_incl)
    e_excl = jnp.exp(l_excl)
    e_inv = jnp.exp(-l_incl)
    w_chunk = e_incl[c - 1:c, :]

    kk_all = k * kk_ref[...]
    k2_all = k * (1.0 + (a - 1.0) * ka_ref[...])
    rk_all = r * k2_all * rk_ref[...]

    def stack(x):
        return jnp.concatenate([jnp.where(head0, x, 0.0), jnp.where(head0, 0.0, x)], axis=0)

    t4 = lax.broadcasted_iota(jnp.int32, (4 * c, 4 * c), 0) & (c - 1)
    s4 = lax.broadcasted_iota(jnp.int32, (4 * c, 4 * c), 1) & (c - 1)
    is_r_row = lax.broadcasted_iota(jnp.int32, (4 * c, 4 * c), 0) >= 2 * c
    causal = (s4 < t4) | (is_r_row & (s4 == t4))

    pairs = range(n_pairs)
    sls = [slice(j * LANES, (j + 1) * LANES) for j in pairs]
    ar_s, bk_s, v_s = [], [], []
    for sl in sls:
        kk = kk_all[:, sl]
        nrm = jnp.sqrt(head_sum(kk * kk))
        kk = kk / jnp.maximum(nrm, 1e-12)
        at = -kk * e_excl[:, sl]
        rt = r[:, sl] * e_incl[:, sl]
        bt = kk * a[:, sl] * e_inv[:, sl]
        kt = k2_all[:, sl] * e_inv[:, sl]
        ar_s.append(jnp.concatenate([stack(at), stack(rt)], axis=0).astype(BF16))
        bk_s.append(jnp.concatenate([stack(bt), stack(kt)], axis=0))
        v_s.append(stack(v[:, sl]))

    m = [jnp.where(causal, _dot(ar_s[j], bk_s[j], _NT), 0.0) for j in pairs]
    t_inv = _unit_lower_inverses([m[j][:2 * c, :2 * c] for j in pairs])
    akv = [_dot(m[j][:2 * c, 2 * c:], v_s[j]) for j in pairs]
    hs = [state_ref[j] for j in pairs]
    arh = [_dot(ar_s[j], hs[j], _NT) for j in pairs]
    u = [_dot(t_inv[j], arh[j][:2 * c] + akv[j]) for j in pairs]
    uv = [jnp.concatenate([u[j], v_s[j]], axis=0) for j in pairs]
    for j in pairs:
        wc = w_chunk[:, sls[j]]
        state_ref[j] = hs[j] * wc + _dot(uv[j], bk_s[j] * wc, _TN)
    y_s = [arh[j][2 * c:] + _dot(m[j][2 * c:, :], uv[j]) for j in pairs]

    for j, sl in enumerate(sls):
        y = y_s[j][:c] + y_s[j][c:]
        mean = head_sum(y) * (1.0 / HEAD_DIM)
        yc = y - mean
        var = head_sum(yc * yc) * (1.0 / HEAD_DIM)
        yn = yc * lax.rsqrt(var + GN_EPS) * lnw_ref[:, sl] + lnb_ref[:, sl]
        bonus = head_sum(rk_all[:, sl]) * v[:, sl]
        o_ref[:, sl] = (yn + bonus) * g[:, sl]


def _rwkv(proj, mu, w0, w2p, a0, a2p, g2, k_k, k_a, r_k, ln_w, ln_b):
    b, s, _ = proj.shape
    w = w0.shape[-1]
    shift_cols = mu.shape[-1]
    c = CHUNK
    row = lambda n: pl.BlockSpec((1, n), lambda bi, ci: (0, 0))
    mat = lambda m, n: pl.BlockSpec((m, n), lambda bi, ci: (0, 0))
    return pl.pallas_call(
        _rwkv_kernel,
        grid=(b, s // c),
        in_specs=[
            pl.BlockSpec((None, c, shift_cols), lambda bi, ci: (bi, ci, 0)),
            row(shift_cols), row(w), mat(LANES, w), row(w), mat(LANES, w), mat(GATE_LORA, w),
            row(w), row(w), row(w), row(w), row(w),
        ],
        out_specs=pl.BlockSpec((None, c, w), lambda bi, ci: (bi, ci, 0)),
        out_shape=jax.ShapeDtypeStruct((b, s, w), F32),
        scratch_shapes=[
            pltpu.VMEM((1, shift_cols), F32),
            pltpu.VMEM((w // LANES, LANES, LANES), F32),
        ],
        compiler_params=pltpu.CompilerParams(
            dimension_semantics=("arbitrary", "arbitrary"), vmem_limit_bytes=VMEM_LIMIT_BYTES),
        name="rwkv7",
    )(proj, mu, w0, w2p, a0, a2p, g2, k_k, k_a, r_k, ln_w, ln_b)


def _attn_kernel(q_ref, k_ref, v_ref, g_ref, o_ref, ob_ref, lse_ref):
    s_len = q_ref.shape[0]
    blk = ATTN_BLOCK
    scale = HEAD_DIM ** -0.5
    lane = lax.broadcasted_iota(jnp.int32, (blk, LANES), 1)
    head0 = lane < HEAD_DIM

    def block(bi, dil, span, start, prev_start, has_prev_static, prev_valid):
        rows = pl.ds(start, blk, stride=dil) if dil > 1 else pl.ds(start, blk)
        q = q_ref[rows, :] * scale
        kc = k_ref[rows, :]
        vc = v_ref[rows, :]
        if has_prev_static:
            prows = pl.ds(prev_start, blk, stride=dil) if dil > 1 else pl.ds(prev_start, blk)
            kcat = jnp.concatenate([k_ref[prows, :], kc], axis=0)
            vcat = jnp.concatenate([v_ref[prows, :], vc], axis=0)
            nk = 2 * blk
            qi = lax.broadcasted_iota(jnp.int32, (blk, nk), 0) + blk
            kj = lax.broadcasted_iota(jnp.int32, (blk, nk), 1)
            rel = qi - kj
            valid = (rel >= 0) & (rel <= span) & (prev_valid * blk + kj >= blk)
        else:
            kcat, vcat, nk = kc, vc, blk
            qi = lax.broadcasted_iota(jnp.int32, (blk, nk), 0)
            kj = lax.broadcasted_iota(jnp.int32, (blk, nk), 1)
            rel = qi - kj
            valid = (rel >= 0) & (rel <= span)
        outs, lses = [], []
        for hmask in (head0, jnp.logical_not(head0)):
            qh = jnp.where(hmask, q, 0.0)
            sc = _dot(qh, kcat, _NT)
            sc = jnp.where(valid, sc, -jnp.inf)
            m = jnp.max(sc, axis=-1, keepdims=True)
            pe = jnp.exp(sc - m)
            den = jnp.sum(pe, axis=-1, keepdims=True)
            outs.append(_dot(pe, vcat) / den)
            lses.append(m + jnp.log(den))
        ob_ref[bi, rows, :] = jnp.where(head0, outs[0], outs[1])
        lse_ref[bi, rows, :] = jnp.where(head0, lses[0], lses[1])

    for bi, (window, dil) in enumerate(DILATED_PAIRS):
        sub_len = s_len // dil
        span = window // dil
        nb = -(-sub_len // blk)
        assert sub_len % blk == 0
        if nb == 1:
            def body(r, carry, bi=bi, dil=dil, span=span):
                block(bi, dil, span, r, 0, False, None)
                return carry
            lax.fori_loop(0, dil, body, 0)
        else:
            def body(i, carry, bi=bi, dil=dil, span=span, nb=nb):
                assert nb & (nb - 1) == 0
                r = lax.shift_right_logical(i, nb.bit_length() - 1)
                n = i & (nb - 1)
                start = n * (blk * dil) + r
                prev_start = jnp.maximum(n - 1, 0) * (blk * dil) + r
                block(bi, dil, span, start, prev_start, True, jnp.minimum(n, 1))
                return carry
            lax.fori_loop(0, dil * nb, body, 0)

    tile = 256
    lane_t = lax.broadcasted_iota(jnp.int32, (tile, LANES), 1)
    head0_t = lane_t < HEAD_DIM

    def mix(i, carry):
        rows = pl.ds(pl.multiple_of(i * tile, tile), tile)
        l0, l1, l2 = lse_ref[0, rows, :], lse_ref[1, rows, :], lse_ref[2, rows, :]
        mx = jnp.maximum(jnp.maximum(l0, l1), l2)
        e0, e1, e2 = jnp.exp(l0 - mx), jnp.exp(l1 - mx), jnp.exp(l2 - mx)
        den = e0 + e1 + e2
        o = (e0 / den) * ob_ref[0, rows, :] + (e1 / den) * ob_ref[1, rows, :] + (e2 / den) * ob_ref[2, rows, :]
        o2 = o * o
        s0 = jnp.sum(jnp.where(head0_t, o2, 0.0), axis=-1, keepdims=True)
        s1 = jnp.sum(jnp.where(head0_t, 0.0, o2), axis=-1, keepdims=True)
        ms = jnp.where(head0_t, s0, s1) * (1.0 / HEAD_DIM)
        o_ref[rows, :] = o * lax.rsqrt(ms + NORM_EPS) * g_ref[...]
        return carry

    lax.fori_loop(0, s_len // tile, mix, 0)


def _attention(proj, out_g, q_col0, width):
    b, s, _ = proj.shape
    n_pairs = width // LANES
    qb = q_col0 // LANES
    spec = lambda off: pl.BlockSpec((None, s, LANES), lambda bi, pi: (bi, 0, off + pi))
    return pl.pallas_call(
        _attn_kernel,
        grid=(b, n_pairs),
        in_specs=[spec(qb), spec(qb + n_pairs), spec(qb + 2 * n_pairs),
                  pl.BlockSpec((1, LANES), lambda bi, pi: (0, pi))],
        out_specs=pl.BlockSpec((None, s, LANES), lambda bi, pi: (bi, 0, pi)),
        out_shape=jax.ShapeDtypeStruct((b, s, width), F32),
        scratch_shapes=[
            pltpu.VMEM((len(DILATED_PAIRS), s, LANES), F32),
            pltpu.VMEM((len(DILATED_PAIRS), s, LANES), F32),
        ],
        compiler_params=pltpu.CompilerParams(
            dimension_semantics=("arbitrary", "arbitrary"), vmem_limit_bytes=VMEM_LIMIT_BYTES),
        name="dilated_attn",
    )(proj, proj, proj, out_g)


def _ffn_kernel(x_ref, ya_ref, yb_ref, wo_ref, gn_ref, wg_ref, wu_ref, wd_ref, gf_ref, o_ref, *, final):
    wa = ya_ref.shape[-1]
    x1 = (x_ref[...]
          + jnp.dot(ya_ref[...].astype(BF16), wo_ref[0:wa, :], preferred_element_type=F32)
          + jnp.dot(yb_ref[...].astype(BF16), wo_ref[wa:, :], preferred_element_type=F32))
    h = _rmsnorm(x1, gn_ref[...]).astype(BF16)
    gate = jnp.dot(h, wg_ref[...], preferred_element_type=F32)
    up = jnp.dot(h, wu_ref[...], preferred_element_type=F32)
    act = (gate * _sigmoid(gate)) * up
    x2 = x1 + jnp.dot(act.astype(BF16), wd_ref[...], preferred_element_type=F32)
    o_ref[...] = _rmsnorm(x2, gf_ref[...]) if final else x2


def _ffn(x2d, ya, yb, wo, gn, wg, wu, wd, gf, tm, final):
    t, d = x2d.shape
    wa, wb = ya.shape[-1], yb.shape[-1]
    dff = wg.shape[-1]
    const = lambda m, n: pl.BlockSpec((m, n), lambda i: (0, 0), pipeline_mode=pl.Buffered(1))
    return pl.pallas_call(
        functools.partial(_ffn_kernel, final=final),
        grid=(t // tm,),
        in_specs=[
            pl.BlockSpec((tm, d), lambda i: (i, 0)),
            pl.BlockSpec((tm, wa), lambda i: (i, 0)),
            pl.BlockSpec((tm, wb), lambda i: (i, 0)),
            const(wa + wb, d), const(1, d), const(d, dff), const(d, dff), const(dff, d), const(1, d),
        ],
        out_specs=pl.BlockSpec((tm, d), lambda i: (i, 0)),
        out_shape=jax.ShapeDtypeStruct((t, d), F32),
        compiler_params=pltpu.CompilerParams(
            dimension_semantics=("arbitrary",), vmem_limit_bytes=VMEM_LIMIT_BYTES),
        name="outproj_ffn",
    )(x2d, ya, yb, wo, gn, wg, wu, wd, gf)


def kernel(x, mix_norm_g, w_in, mu_shift, decay_w0, decay_w2, iclr_a0, iclr_a2, gate_g2, k_k, k_a, r_k,
           ln_x_w, ln_x_b, attn_out_g, w_out, ffn_norm_g, w_gate, w_up, w_down, final_norm_g):
    b, s, d = x.shape
    depth = w_in.shape[0]
    w = decay_w0.shape[-1]
    shift_cols = mu_shift.shape[-1]
    attn_w = attn_out_g.shape[-1]
    assert shift_cols == 3 * w + DECAY_LORA + AAA_LORA + GATE_LORA
    assert DECAY_LORA + AAA_LORA == LANES and shift_cols % LANES == 0
    tm = 256
    x2d = x.reshape(b * s, d)
    for i in range(depth):
        proj = _inproj(x2d, mix_norm_g[i][None], w_in[i].astype(BF16), tm).reshape(b, s, -1)
        zeros = jnp.zeros((LANES - DECAY_LORA, w), F32)
        w2p = jnp.concatenate([decay_w2[i], zeros], axis=0)
        a2p = jnp.concatenate([zeros, iclr_a2[i]], axis=0)
        y_a = _rwkv(proj, mu_shift[i][None], decay_w0[i][None], w2p, iclr_a0[i][None], a2p, gate_g2[i],
                    k_k[i][None], k_a[i][None], r_k[i].reshape(1, w), ln_x_w[i][None], ln_x_b[i][None])
        y_b = _attention(proj, attn_out_g[i][None], shift_cols, attn_w)
        x2d = _ffn(x2d, y_a.reshape(b * s, w), y_b.reshape(b * s, attn_w), w_out[i].astype(BF16),
                   ffn_norm_g[i][None], w_gate[i].astype(BF16), w_up[i].astype(BF16),
                   w_down[i].astype(BF16), final_norm_g[None], tm, final=(i == depth - 1))
    return x2d.reshape(b, s, d)
```

```python
import functools

import jax
import jax.numpy as jnp
from jax import lax
from jax.experimental import pallas as pl
from jax.experimental.pallas import tpu as pltpu

HEAD_DIM = 64
LANES = 128
DECAY_LORA = 64
AAA_LORA = 64
GATE_LORA = 128
DILATED_PAIRS = ((128, 1), (512, 4), (2048, 16))
ATTN_BLOCK = 128
NORM_EPS = 1e-6
GN_EPS = 64e-5
CHUNK = 64
VMEM_LIMIT_BYTES = 56 * 1024 * 1024

F32 = jnp.float32
BF16 = jnp.bfloat16


def _dot(a, b, dims=((1,), (0,))):
    return lax.dot_general(a.astype(BF16), b.astype(BF16), (dims, ((), ())),
                           preferred_element_type=F32)


_NT = ((1,), (1,))
_TN = ((0,), (0,))


def _rmsnorm(x, g):
    return x * lax.rsqrt(jnp.mean(x * x, axis=-1, keepdims=True) + NORM_EPS) * g


def _sigmoid(x):
    return 1.0 / (1.0 + jnp.exp(-x))


def _inproj_kernel(x_ref, g_ref, w_ref, o_ref):
    h = _rmsnorm(x_ref[...], g_ref[...])
    o_ref[...] = jnp.dot(h.astype(BF16), w_ref[...], preferred_element_type=F32)


def _inproj(x2d, g, w_bf16, tm):
    t, d = x2d.shape
    n = w_bf16.shape[1]
    return pl.pallas_call(
        _inproj_kernel,
        grid=(t // tm,),
        in_specs=[
            pl.BlockSpec((tm, d), lambda i: (i, 0)),
            pl.BlockSpec((1, d), lambda i: (0, 0)),
            pl.BlockSpec((d, n), lambda i: (0, 0)),
        ],
        out_specs=pl.BlockSpec((tm, n), lambda i: (i, 0)),
        out_shape=jax.ShapeDtypeStruct((t, n), F32),
        compiler_params=pltpu.CompilerParams(
            dimension_semantics=("arbitrary",), vmem_limit_bytes=VMEM_LIMIT_BYTES),
        name="inproj",
    )(x2d, g, w_bf16)


def _unit_lower_inverses(n_mats):
    c = n_mats[0].shape[0]
    ri = lax.broadcasted_iota(jnp.int32, (c, c), 0)
    ci = lax.broadcasted_iota(jnp.int32, (c, c), 1)
    same16 = (ri >> 4) == (ci >> 4)
    same32 = (ri >> 5) == (ci >> 5)
    eye = jnp.where(ri == ci, 1.0, 0.0).astype(F32)
    p = [jnp.where(same16, n, 0.0) for n in n_mats]
    t = [eye + n for n in p]
    for _ in range(3):
        p = [_dot(x, x) for x in p]
        t = [x + _dot(x, y) for x, y in zip(t, p)]
    for off in ([jnp.where(same32 & jnp.logical_not(same16), n, 0.0) for n in n_mats],
                [jnp.where(same32, 0.0, n) for n in n_mats]):
        q = [_dot(x, n) for x, n in zip(t, off)]
        t = [x + _dot(y, x) for x, y in zip(t, q)]
    return t


def _rwkv_kernel(p_ref, mu_ref, w0_ref, w2_ref, a0_ref, a2_ref, g2_ref, kk_ref, ka_ref, rk_ref,
                 lnw_ref, lnb_ref, o_ref, carry_ref, state_ref):
    c = CHUNK
    w = w0_ref.shape[-1]
    n_pairs = w // LANES

    @pl.when(pl.program_id(1) == 0)
    def _():
        carry_ref[...] = jnp.zeros_like(carry_ref)
        state_ref[...] = jnp.zeros_like(state_ref)

    p = p_ref[...]
    row = lax.broadcasted_iota(jnp.int32, p.shape, 0)
    prev = jnp.where(row == 0, carry_ref[...], pltpu.roll(p, 1, axis=0))
    carry_ref[...] = p[c - 1:c, :]
    p = p + (prev - p) * mu_ref[...]

    r = p[:, 0:w]
    k = p[:, w:2 * w]
    v = p[:, 2 * w:3 * w]
    xwa = p[:, 3 * w:3 * w + LANES]
    xg = p[:, 3 * w + LANES:3 * w + 2 * LANES]

    z = w0_ref[...] + _dot(jnp.tanh(xwa), w2_ref[...])
    wlog = -(jnp.maximum(-z, 0.0) + jnp.log(1.0 + jnp.exp(-jnp.abs(z)))) - 0.5
    ld = -jnp.exp(wlog)
    a = _sigmoid(a0_ref[...] + _dot(xwa, a2_ref[...]))
    g = _dot(_sigmoid(xg), g2_ref[...])

    lane = lax.broadcasted_iota(jnp.int32, (c, LANES), 1)
    head0 = lane < HEAD_DIM

    def head_sum(x):
        s0 = jnp.sum(jnp.where(head0, x, 0.0), axis=-1, keepdims=True)
        s1 = jnp.sum(jnp.where(head0, 0.0, x), axis=-1, keepdims=True)
        return jnp.where(head0, s0, s1)

    ti = lax.broadcasted_iota(jnp.int32, (c, c), 0)
    si = lax.broadcasted_iota(jnp.int32, (c, c), 1)
    tri = jnp.where(si <= ti, 1.0, 0.0).astype(BF16)
    ld_hi = ld.astype(BF16)
    ld_r1 = ld - ld_hi.astype(F32)
    ld_mid = ld_r1.astype(BF16)
    ld_lo = (ld_r1 - ld_mid.astype(F32)).astype(BF16)
    l_incl = _dot(tri, ld_hi) + _dot(tri, ld_mid) + _dot(tri, ld_lo)
    l_excl = l_incl - ld
    e_incl = jnp.exp(l_incl)
    e_excl = jnp.exp(l_excl)
    e_inv = jnp.exp(-l_incl)
    w_chunk = e_incl[c - 1:c, :]

    kk_all = k * kk_ref[...]
    k2_all = k * (1.0 + (a - 1.0) * ka_ref[...])
    rk_all = r * k2_all * rk_ref[...]

    def stack(x):
        return jnp.concatenate([jnp.where(head0, x, 0.0), jnp.where(head0, 0.0, x)], axis=0)

    t4 = lax.broadcasted_iota(jnp.int32, (4 * c, 4 * c), 0) & (c - 1)
    s4 = lax.broadcasted_iota(jnp.int32, (4 * c, 4 * c), 1) & (c - 1)
    is_r_row = lax.broadcasted_iota(jnp.int32, (4 * c, 4 * c), 0) >= 2 * c
    causal = (s4 < t4) | (is_r_row & (s4 == t4))

    pairs = range(n_pairs)
    sls = [slice(j * LANES, (j + 1) * LANES) for j in pairs]
    ar_s, bk_s, v_s = [], [], []
    for sl in sls:
        kk = kk_all[:, sl]
        nrm = jnp.sqrt(head_sum(kk * kk))
        kk = kk / jnp.maximum(nrm, 1e-12)
        at = -kk * e_excl[:, sl]
        rt = r[:, sl] * e_incl[:, sl]
        bt = kk * a[:, sl] * e_inv[:, sl]
        kt = k2_all[:, sl] * e_inv[:, sl]
        ar_s.append(jnp.concatenate([stack(at), stack(rt)], axis=0).astype(BF16))
        bk_s.append(jnp.concatenate([stack(bt), stack(kt)], axis=0))
        v_s.append(stack(v[:, sl]))

    m = [jnp.where(causal, _dot(ar_s[j], bk_s[j], _NT), 0.0) for j in pairs]
    t_inv = _unit_lower_inverses([m[j][:2 * c, :2 * c] for j in pairs])
    akv = [_dot(m[j][:2 * c, 2 * c:], v_s[j]) for j in pairs]
    hs = [state_ref[j] for j in pairs]
    arh = [_dot(ar_s[j], hs[j], _NT) for j in pairs]
    u = [_dot(t_inv[j], arh[j][:2 * c] + akv[j]) for j in pairs]
    uv = [jnp.concatenate([u[j], v_s[j]], axis=0) for j in pairs]
    for j in pairs:
        wc = w_chunk[:, sls[j]]
        state_ref[j] = hs[j] * wc + _dot(uv[j], bk_s[j] * wc, _TN)
    y_s = [arh[j][2 * c:] + _dot(m[j][2 * c:, :], uv[j]) for j in pairs]

    for j, sl in enumerate(sls):
        y = y_s[j][:c] + y_s[j][c:]
        mean = head_sum(y) * (1.0 / HEAD_DIM)
        yc = y - mean
        var = head_sum(yc * yc) * (1.0 / HEAD_DIM)
        yn = yc * lax.rsqrt(var + GN_EPS) * lnw_ref[:, sl] + lnb_ref[:, sl]
        bonus = head_sum(rk_all[:, sl]) * v[:, sl]
        o_ref[:, sl] = (yn + bonus) * g[:, sl]


def _rwkv(proj, mu, w0, w2p, a0, a2p, g2, k_k, k_a, r_k, ln_w, ln_b):
    b, s, _ = proj.shape
    w = w0.shape[-1]
    shift_cols = mu.shape[-1]
    c = CHUNK
    row = lambda n: pl.BlockSpec((1, n), lambda bi, ci: (0, 0))
    mat = lambda m, n: pl.BlockSpec((m, n), lambda bi, ci: (0, 0))
    return pl.pallas_call(
        _rwkv_kernel,
        grid=(b, s // c),
        in_specs=[
            pl.BlockSpec((None, c, shift_cols), lambda bi, ci: (bi, ci, 0)),
            row(shift_cols), row(w), mat(LANES, w), row(w), mat(LANES, w), mat(GATE_LORA, w),
            row(w), row(w), row(w), row(w), row(w),
        ],
        out_specs=pl.BlockSpec((None, c, w), lambda bi, ci: (bi, ci, 0)),
        out_shape=jax.ShapeDtypeStruct((b, s, w), F32),
        scratch_shapes=[
            pltpu.VMEM((1, shift_cols), F32),
            pltpu.VMEM((w // LANES, LANES, LANES), F32),
        ],
        compiler_params=pltpu.CompilerParams(
            dimension_semantics=("arbitrary", "arbitrary"), vmem_limit_bytes=VMEM_LIMIT_BYTES),
        name="rwkv7",
    )(proj, mu, w0, w2p, a0, a2p, g2, k_k, k_a, r_k, ln_w, ln_b)


def _attn_kernel(q_ref, k_ref, v_ref, g_ref, o_ref, ob_ref, lse_ref):
    s_len = q_ref.shape[0]
    blk = ATTN_BLOCK
    scale = HEAD_DIM ** -0.5
    lane = lax.broadcasted_iota(jnp.int32, (blk, LANES), 1)
    head0 = lane < HEAD_DIM

    def stack(x):
        return jnp.concatenate([jnp.where(head0, x, 0.0), jnp.where(head0, 0.0, x)], axis=0)

    def run_blocks(bi, dil, span, blocks, has_prev):
        ds = lambda s: pl.ds(s, blk, stride=dil) if dil > 1 else pl.ds(s, blk)
        nk = 2 * blk if has_prev else blk
        qi = lax.broadcasted_iota(jnp.int32, (2 * blk, nk), 0) & (blk - 1)
        kj = lax.broadcasted_iota(jnp.int32, (2 * blk, nk), 1)
        rel = qi - kj + (blk if has_prev else 0)
        in_window = (rel >= 0) & (rel <= span)
        rows = [ds(start) for start, _, _ in blocks]
        q_s = [stack(q_ref[r, :] * scale).astype(BF16) for r in rows]
        if has_prev:
            kcat = [jnp.concatenate([k_ref[ds(ps), :], k_ref[r, :]], axis=0).astype(BF16)
                    for r, (_, ps, _) in zip(rows, blocks)]
            vcat = [jnp.concatenate([v_ref[ds(ps), :], v_ref[r, :]], axis=0).astype(BF16)
                    for r, (_, ps, _) in zip(rows, blocks)]
            valid = [in_window & (pv * blk + kj >= blk) for _, _, pv in blocks]
        else:
            kcat = [k_ref[r, :].astype(BF16) for r in rows]
            vcat = [v_ref[r, :].astype(BF16) for r in rows]
            valid = [in_window for _ in blocks]
        sc = [jnp.where(ok, _dot(q, kc, _NT), -jnp.inf) for q, kc, ok in zip(q_s, kcat, valid)]
        mx = [jnp.max(x, axis=-1, keepdims=True) for x in sc]
        pe = [jnp.exp(x - m) for x, m in zip(sc, mx)]
        den = [jnp.sum(x, axis=-1, keepdims=True) for x in pe]
        o_s = [_dot(x, vc) / d for x, vc, d in zip(pe, vcat, den)]
        lse = [m + jnp.log(d) for m, d in zip(mx, den)]
        for r, o, l in zip(rows, o_s, lse):
            ob_ref[bi, r, :] = jnp.where(head0, o[:blk], o[blk:])
            lse_ref[bi, r, :] = jnp.where(head0, l[:blk], l[blk:])

    group = 4
    for bi, (window, dil) in enumerate(DILATED_PAIRS):
        sub_len = s_len // dil
        span = window // dil
        assert sub_len % blk == 0
        nb = sub_len // blk
        if nb == 1:
            assert dil % group == 0

            def body(i, carry, bi=bi, dil=dil, span=span):
                run_blocks(bi, dil, span, [(i * group + g, None, None) for g in range(group)], False)
                return carry
            lax.fori_loop(0, dil // group, body, 0)
        elif nb == group:
            def body(r, carry, bi=bi, dil=dil, span=span):
                blocks = [(n * blk * dil + r, max(n - 1, 0) * blk * dil + r, min(n, 1)) for n in range(group)]
                run_blocks(bi, dil, span, blocks, True)
                return carry
            lax.fori_loop(0, dil, body, 0)
        else:
            assert dil == 1 and nb % group == 0

            def body(i, carry, bi=bi, dil=dil, span=span):
                n0 = i * group
                blocks = [(pl.multiple_of((n0 + g) * blk, blk),
                           pl.multiple_of(jnp.maximum(n0 + g - 1, 0) * blk, blk),
                           jnp.minimum(n0 + g, 1)) for g in range(group)]
                run_blocks(bi, dil, span, blocks, True)
                return carry
            lax.fori_loop(0, nb // group, body, 0)

    tile = 256
    lane_t = lax.broadcasted_iota(jnp.int32, (tile, LANES), 1)
    head0_t = lane_t < HEAD_DIM

    def mix(i, carry):
        rows = pl.ds(pl.multiple_of(i * tile, tile), tile)
        l0, l1, l2 = lse_ref[0, rows, :], lse_ref[1, rows, :], lse_ref[2, rows, :]
        mx = jnp.maximum(jnp.maximum(l0, l1), l2)
        e0, e1, e2 = jnp.exp(l0 - mx), jnp.exp(l1 - mx), jnp.exp(l2 - mx)
        den = e0 + e1 + e2
        o = (e0 / den) * ob_ref[0, rows, :] + (e1 / den) * ob_ref[1, rows, :] + (e2 / den) * ob_ref[2, rows, :]
        o2 = o * o
        s0 = jnp.sum(jnp.where(head0_t, o2, 0.0), axis=-1, keepdims=True)
        s1 = jnp.sum(jnp.where(head0_t, 0.0, o2), axis=-1, keepdims=True)
        ms = jnp.where(head0_t, s0, s1) * (1.0 / HEAD_DIM)
        o_ref[rows, :] = o * lax.rsqrt(ms + NORM_EPS) * g_ref[...]
        return carry

    lax.fori_loop(0, s_len // tile, mix, 0)


def _attention(proj, out_g, q_col0, width):
    b, s, _ = proj.shape
    n_pairs = width // LANES
    qb = q_col0 // LANES
    spec = lambda off: pl.BlockSpec((None, s, LANES), lambda bi, pi: (bi, 0, off + pi))
    return pl.pallas_call(
        _attn_kernel,
        grid=(b, n_pairs),
        in_specs=[spec(qb), spec(qb + n_pairs), spec(qb + 2 * n_pairs),
                  pl.BlockSpec((1, LANES), lambda bi, pi: (0, pi))],
        out_specs=pl.BlockSpec((None, s, LANES), lambda bi, pi: (bi, 0, pi)),
        out_shape=jax.ShapeDtypeStruct((b, s, width), F32),
        scratch_shapes=[
            pltpu.VMEM((len(DILATED_PAIRS), s, LANES), F32),
            pltpu.VMEM((len(DILATED_PAIRS), s, LANES), F32),
        ],
        compiler_params=pltpu.CompilerParams(
            dimension_semantics=("arbitrary", "arbitrary"), vmem_limit_bytes=VMEM_LIMIT_BYTES),
        name="dilated_attn",
    )(proj, proj, proj, out_g)


def _ffn_kernel(x_ref, ya_ref, yb_ref, wo_ref, gn_ref, wg_ref, wu_ref, wd_ref, gf_ref, o_ref, *, final):
    wa = ya_ref.shape[-1]
    x1 = (x_ref[...]
          + jnp.dot(ya_ref[...].astype(BF16), wo_ref[0:wa, :], preferred_element_type=F32)
          + jnp.dot(yb_ref[...].astype(BF16), wo_ref[wa:, :], preferred_element_type=F32))
    h = _rmsnorm(x1, gn_ref[...]).astype(BF16)
    gate = jnp.dot(h, wg_ref[...], preferred_element_type=F32)
    up = jnp.dot(h, wu_ref[...], preferred_element_type=F32)
    act = (gate * _sigmoid(gate)) * up
    x2 = x1 + jnp.dot(act.astype(BF16), wd_ref[...], preferred_element_type=F32)
    o_ref[...] = _rmsnorm(x2, gf_ref[...]) if final else x2


def _ffn(x2d, ya, yb, wo, gn, wg, wu, wd, gf, tm, final):
    t, d = x2d.shape
    wa, wb = ya.shape[-1], yb.shape[-1]
    dff = wg.shape[-1]
    const = lambda m, n: pl.BlockSpec((m, n), lambda i: (0, 0), pipeline_mode=pl.Buffered(1))
    return pl.pallas_call(
        functools.partial(_ffn_kernel, final=final),
        grid=(t // tm,),
        in_specs=[
            pl.BlockSpec((tm, d), lambda i: (i, 0)),
            pl.BlockSpec((tm, wa), lambda i: (i, 0)),
            pl.BlockSpec((tm, wb), lambda i: (i, 0)),
            const(wa + wb, d), const(1, d), const(d, dff), const(d, dff), const(dff, d), const(1, d),
        ],
        out_specs=pl.BlockSpec((tm, d), lambda i: (i, 0)),
        out_shape=jax.ShapeDtypeStruct((t, d), F32),
        compiler_params=pltpu.CompilerParams(
            dimension_semantics=("arbitrary",), vmem_limit_bytes=VMEM_LIMIT_BYTES),
        name="outproj_ffn",
    )(x2d, ya, yb, wo, gn, wg, wu, wd, gf)


def kernel(x, mix_norm_g, w_in, mu_shift, decay_w0, decay_w2, iclr_a0, iclr_a2, gate_g2, k_k, k_a, r_k,
           ln_x_w, ln_x_b, attn_out_g, w_out, ffn_norm_g, w_gate, w_up, w_down, final_norm_g):
    b, s, d = x.shape
    depth = w_in.shape[0]
    w = decay_w0.shape[-1]
    shift_cols = mu_shift.shape[-1]
    attn_w = attn_out_g.shape[-1]
    assert shift_cols == 3 * w + DECAY_LORA + AAA_LORA + GATE_LORA
    assert DECAY_LORA + AAA_LORA == LANES and shift_cols % LANES == 0
    tm = 256
    x2d = x.reshape(b * s, d)
    for i in range(depth):
        proj = _inproj(x2d, mix_norm_g[i][None], w_in[i].astype(BF16), tm).reshape(b, s, -1)
        zeros = jnp.zeros((LANES - DECAY_LORA, w), F32)
        w2p = jnp.concatenate([decay_w2[i], zeros], axis=0)
        a2p = jnp.concatenate([zeros, iclr_a2[i]], axis=0)
        y_a = _rwkv(proj, mu_shift[i][None], decay_w0[i][None], w2p, iclr_a0[i][None], a2p, gate_g2[i],
                    k_k[i][None], k_a[i][None], r_k[i].reshape(1, w), ln_x_w[i][None], ln_x_b[i][None])
        y_b = _attention(proj, attn_out_g[i][None], shift_cols, attn_w)
        x2d = _ffn(x2d, y_a.reshape(b * s, w), y_b.reshape(b * s, attn_w), w_out[i].astype(BF16),
                   ffn_norm_g[i][None], w_gate[i].astype(BF16), w_up[i].astype(BF16),
                   w_down[i].astype(BF16), final_norm_g[None], tm, final=(i == depth - 1))
    return x2d.reshape(b, s, d)
```

```python
import functools

import jax
import jax.numpy as jnp
from jax import lax
from jax.experimental import pallas as pl
from jax.experimental.pallas import tpu as pltpu

HEAD_DIM = 64
LANES = 128
DECAY_LORA = 64
AAA_LORA = 64
GATE_LORA = 128
DILATED_PAIRS = ((128, 1), (512, 4), (2048, 16))
ATTN_BLOCK = 128
NORM_EPS = 1e-6
GN_EPS = 64e-5
CHUNK = 64
VMEM_LIMIT_BYTES = 56 * 1024 * 1024

F32 = jnp.float32
BF16 = jnp.bfloat16


def _dot(a, b, dims=((1,), (0,))):
    return lax.dot_general(a.astype(BF16), b.astype(BF16), (dims, ((), ())),
                           preferred_element_type=F32)


_NT = ((1,), (1,))
_TN = ((0,), (0,))


def _rmsnorm(x, g):
    return x * lax.rsqrt(jnp.mean(x * x, axis=-1, keepdims=True) + NORM_EPS) * g


def _sigmoid(x):
    return 1.0 / (1.0 + jnp.exp(-x))


def _inproj_kernel(x_ref, g_ref, w_ref, o_ref):
    h = _rmsnorm(x_ref[...], g_ref[...])
    o_ref[...] = jnp.dot(h.astype(BF16), w_ref[...], preferred_element_type=F32)


def _inproj(x2d, g, w_bf16, tm):
    t, d = x2d.shape
    n = w_bf16.shape[1]
    return pl.pallas_call(
        _inproj_kernel,
        grid=(t // tm,),
        in_specs=[
            pl.BlockSpec((tm, d), lambda i: (i, 0)),
            pl.BlockSpec((1, d), lambda i: (0, 0)),
            pl.BlockSpec((d, n), lambda i: (0, 0)),
        ],
        out_specs=pl.BlockSpec((tm, n), lambda i: (i, 0)),
        out_shape=jax.ShapeDtypeStruct((t, n), F32),
        compiler_params=pltpu.CompilerParams(
            dimension_semantics=("arbitrary",), vmem_limit_bytes=VMEM_LIMIT_BYTES),
        name="inproj",
    )(x2d, g, w_bf16)


def _unit_lower_inverses(n_mats):
    c = n_mats[0].shape[0]
    ri = lax.broadcasted_iota(jnp.int32, (c, c), 0)
    ci = lax.broadcasted_iota(jnp.int32, (c, c), 1)
    same16 = (ri >> 4) == (ci >> 4)
    same32 = (ri >> 5) == (ci >> 5)
    eye = jnp.where(ri == ci, 1.0, 0.0).astype(F32)
    p = [jnp.where(same16, n, 0.0) for n in n_mats]
    t = [eye + n for n in p]
    for _ in range(3):
        p = [_dot(x, x) for x in p]
        t = [x + _dot(x, y) for x, y in zip(t, p)]
    for off in ([jnp.where(same32 & jnp.logical_not(same16), n, 0.0) for n in n_mats],
                [jnp.where(same32, 0.0, n) for n in n_mats]):
        q = [_dot(x, n) for x, n in zip(t, off)]
        t = [x + _dot(y, x) for x, y in zip(t, q)]
    return t


def _rwkv_kernel(p_ref, mu_ref, w0_ref, w2_ref, a0_ref, a2_ref, g2_ref, kk_ref, ka_ref, rk_ref,
                 lnw_ref, lnb_ref, o_ref, carry_ref, state_ref):
    c = CHUNK
    nrows = p_ref.shape[0]
    w = w0_ref.shape[-1]
    n_pairs = w // LANES
    rc = nrows * c

    @pl.when(pl.program_id(1) == 0)
    def _():
        carry_ref[...] = jnp.zeros_like(carry_ref)
        state_ref[...] = jnp.zeros_like(state_ref)

    first = lax.broadcasted_iota(jnp.int32, (c, p_ref.shape[-1]), 0) == 0
    shifted = []
    for b in range(nrows):
        pb = p_ref[b]
        prev = jnp.where(first, carry_ref[b:b + 1, :], pltpu.roll(pb, 1, axis=0))
        carry_ref[b:b + 1, :] = pb[c - 1:c, :]
        shifted.append(pb + (prev - pb) * mu_ref[...])
    p = jnp.concatenate(shifted, axis=0)

    r = p[:, 0:w]
    k = p[:, w:2 * w]
    v = p[:, 2 * w:3 * w]
    xwa = p[:, 3 * w:3 * w + LANES]
    xg = p[:, 3 * w + LANES:3 * w + 2 * LANES]

    z = w0_ref[...] + _dot(jnp.tanh(xwa), w2_ref[...])
    wlog = -(jnp.maximum(-z, 0.0) + jnp.log(1.0 + jnp.exp(-jnp.abs(z)))) - 0.5
    ld = -jnp.exp(wlog)
    a = _sigmoid(a0_ref[...] + _dot(xwa, a2_ref[...]))
    g = _dot(_sigmoid(xg), g2_ref[...])

    first_head = lambda n: lax.broadcasted_iota(jnp.int32, (n, LANES), 1) < HEAD_DIM
    head0 = first_head(c)

    def head_sum(x):
        h0 = first_head(x.shape[0])
        s0 = jnp.sum(jnp.where(h0, x, 0.0), axis=-1, keepdims=True)
        s1 = jnp.sum(jnp.where(h0, 0.0, x), axis=-1, keepdims=True)
        return jnp.where(h0, s0, s1)

    ti = lax.broadcasted_iota(jnp.int32, (rc, rc), 0)
    si = lax.broadcasted_iota(jnp.int32, (rc, rc), 1)
    assert c == 64
    tri = jnp.where((si <= ti) & ((si >> 6) == (ti >> 6)), 1.0, 0.0).astype(BF16)
    ld_hi = ld.astype(BF16)
    ld_r1 = ld - ld_hi.astype(F32)
    ld_mid = ld_r1.astype(BF16)
    ld_lo = (ld_r1 - ld_mid.astype(F32)).astype(BF16)
    l_incl = _dot(tri, ld_hi) + _dot(tri, ld_mid) + _dot(tri, ld_lo)
    e_incl = jnp.exp(l_incl)
    e_excl = jnp.exp(l_incl - ld)
    e_inv = jnp.exp(-l_incl)

    kk_all = k * kk_ref[...]
    k2_all = k * (1.0 + (a - 1.0) * ka_ref[...])
    rk_all = r * k2_all * rk_ref[...]
    at_all, bt_all = [], []
    for j in range(n_pairs):
        sl = slice(j * LANES, (j + 1) * LANES)
        kk = kk_all[:, sl]
        kk = kk / jnp.maximum(jnp.sqrt(head_sum(kk * kk)), 1e-12)
        at_all.append(-kk * e_excl[:, sl])
        bt_all.append(kk * a[:, sl] * e_inv[:, sl])
    rt_all = r * e_incl
    kt_all = k2_all * e_inv

    def stack(x):
        return jnp.concatenate([jnp.where(head0, x, 0.0), jnp.where(head0, 0.0, x)], axis=0)

    t4 = lax.broadcasted_iota(jnp.int32, (4 * c, 4 * c), 0) & (c - 1)
    s4 = lax.broadcasted_iota(jnp.int32, (4 * c, 4 * c), 1) & (c - 1)
    is_r_row = lax.broadcasted_iota(jnp.int32, (4 * c, 4 * c), 0) >= 2 * c
    causal = (s4 < t4) | (is_r_row & (s4 == t4))

    units = [(b, j) for b in range(nrows) for j in range(n_pairs)]
    rows = lambda b: slice(b * c, (b + 1) * c)
    lanes = lambda j: slice(j * LANES, (j + 1) * LANES)
    ar_s = [jnp.concatenate([stack(at_all[j][rows(b)]), stack(rt_all[rows(b), lanes(j)])], axis=0).astype(BF16)
            for b, j in units]
    bk_s = [jnp.concatenate([stack(bt_all[j][rows(b)]), stack(kt_all[rows(b), lanes(j)])], axis=0)
            for b, j in units]
    v_s = [stack(v[rows(b), lanes(j)]) for b, j in units]
    n_u = range(len(units))

    m = [jnp.where(causal, _dot(ar_s[i], bk_s[i], _NT), 0.0) for i in n_u]
    t_inv = _unit_lower_inverses([m[i][:2 * c, :2 * c] for i in n_u])
    akv = [_dot(m[i][:2 * c, 2 * c:], v_s[i]) for i in n_u]
    hs = [state_ref[i] for i in n_u]
    arh = [_dot(ar_s[i], hs[i], _NT) for i in n_u]
    u = [_dot(t_inv[i], arh[i][:2 * c] + akv[i]) for i in n_u]
    uv = [jnp.concatenate([u[i], v_s[i]], axis=0) for i in n_u]
    for i, (b, j) in enumerate(units):
        wc = e_incl[b * c + c - 1:(b + 1) * c, lanes(j)]
        state_ref[i] = hs[i] * wc + _dot(uv[i], bk_s[i] * wc, _TN)
    y_s = [arh[i][2 * c:] + _dot(m[i][2 * c:, :], uv[i]) for i in n_u]

    for i, (b, j) in enumerate(units):
        sl = lanes(j)
        y = y_s[i][:c] + y_s[i][c:]
        mean = head_sum(y) * (1.0 / HEAD_DIM)
        yc = y - mean
        var = head_sum(yc * yc) * (1.0 / HEAD_DIM)
        yn = yc * lax.rsqrt(var + GN_EPS) * lnw_ref[:, sl] + lnb_ref[:, sl]
        bonus = head_sum(rk_all[rows(b), sl]) * v[rows(b), sl]
        o_ref[b, :, sl] = (yn + bonus) * g[rows(b), sl]


def _rwkv(proj, mu, w0, w2p, a0, a2p, g2, k_k, k_a, r_k, ln_w, ln_b, nrows):
    b, s, _ = proj.shape
    w = w0.shape[-1]
    shift_cols = mu.shape[-1]
    c = CHUNK
    row = lambda n: pl.BlockSpec((1, n), lambda bi, ci: (0, 0))
    mat = lambda m, n: pl.BlockSpec((m, n), lambda bi, ci: (0, 0))
    return pl.pallas_call(
        _rwkv_kernel,
        grid=(b // nrows, s // c),
        in_specs=[
            pl.BlockSpec((nrows, c, shift_cols), lambda bi, ci: (bi, ci, 0)),
            row(shift_cols), row(w), mat(LANES, w), row(w), mat(LANES, w), mat(GATE_LORA, w),
            row(w), row(w), row(w), row(w), row(w),
        ],
        out_specs=pl.BlockSpec((nrows, c, w), lambda bi, ci: (bi, ci, 0)),
        out_shape=jax.ShapeDtypeStruct((b, s, w), F32),
        scratch_shapes=[
            pltpu.VMEM((nrows, shift_cols), F32),
            pltpu.VMEM((nrows * (w // LANES), LANES, LANES), F32),
        ],
        compiler_params=pltpu.CompilerParams(
            dimension_semantics=("arbitrary", "arbitrary"), vmem_limit_bytes=VMEM_LIMIT_BYTES),
        name="rwkv7",
    )(proj, mu, w0, w2p, a0, a2p, g2, k_k, k_a, r_k, ln_w, ln_b)


def _attn_kernel(q_ref, k_ref, v_ref, g_ref, o_ref, ob_ref, lse_ref):
    s_len = q_ref.shape[0]
    blk = ATTN_BLOCK
    scale = HEAD_DIM ** -0.5
    lane = lax.broadcasted_iota(jnp.int32, (blk, LANES), 1)
    head0 = lane < HEAD_DIM

    def stack(x):
        return jnp.concatenate([jnp.where(head0, x, 0.0), jnp.where(head0, 0.0, x)], axis=0)

    def run_blocks(bi, dil, span, blocks, has_prev):
        ds = lambda s: pl.ds(s, blk, stride=dil) if dil > 1 else pl.ds(s, blk)
        nk = 2 * blk if has_prev else blk
        qi = lax.broadcasted_iota(jnp.int32, (2 * blk, nk), 0) & (blk - 1)
        kj = lax.broadcasted_iota(jnp.int32, (2 * blk, nk), 1)
        rel = qi - kj + (blk if has_prev else 0)
        in_window = (rel >= 0) & (rel <= span)
        rows = [ds(start) for start, _, _ in blocks]
        q_s = [stack(q_ref[r, :] * scale).astype(BF16) for r in rows]
        if has_prev:
            kcat = [jnp.concatenate([k_ref[ds(ps), :], k_ref[r, :]], axis=0).astype(BF16)
                    for r, (_, ps, _) in zip(rows, blocks)]
            vcat = [jnp.concatenate([v_ref[ds(ps), :], v_ref[r, :]], axis=0).astype(BF16)
                    for r, (_, ps, _) in zip(rows, blocks)]
            valid = [in_window & (pv * blk + kj >= blk) for _, _, pv in blocks]
        else:
            kcat = [k_ref[r, :].astype(BF16) for r in rows]
            vcat = [v_ref[r, :].astype(BF16) for r in rows]
            valid = [in_window for _ in blocks]
        sc = [jnp.where(ok, _dot(q, kc, _NT), -jnp.inf) for q, kc, ok in zip(q_s, kcat, valid)]
        mx = [jnp.max(x, axis=-1, keepdims=True) for x in sc]
        pe = [jnp.exp(x - m) for x, m in zip(sc, mx)]
        den = [jnp.sum(x, axis=-1, keepdims=True) for x in pe]
        o_s = [_dot(x, vc) / d for x, vc, d in zip(pe, vcat, den)]
        lse = [m + jnp.log(d) for m, d in zip(mx, den)]
        for r, o, l in zip(rows, o_s, lse):
            ob_ref[bi, r, :] = jnp.where(head0, o[:blk], o[blk:])
            lse_ref[bi, r, :] = jnp.where(head0, l[:blk], l[blk:])

    group = 4
    for bi, (window, dil) in enumerate(DILATED_PAIRS):
        sub_len = s_len // dil
        span = window // dil
        assert sub_len % blk == 0
        nb = sub_len // blk
        if nb == 1:
            assert dil % group == 0

            def body(i, carry, bi=bi, dil=dil, span=span):
                run_blocks(bi, dil, span, [(i * group + g, None, None) for g in range(group)], False)
                return carry
            lax.fori_loop(0, dil // group, body, 0)
        elif nb == group:
            def body(r, carry, bi=bi, dil=dil, span=span):
                blocks = [(n * blk * dil + r, max(n - 1, 0) * blk * dil + r, min(n, 1)) for n in range(group)]
                run_blocks(bi, dil, span, blocks, True)
                return carry
            lax.fori_loop(0, dil, body, 0)
        else:
            assert dil == 1 and nb % group == 0

            def body(i, carry, bi=bi, dil=dil, span=span):
                n0 = i * group
                blocks = [(pl.multiple_of((n0 + g) * blk, blk),
                           pl.multiple_of(jnp.maximum(n0 + g - 1, 0) * blk, blk),
                           jnp.minimum(n0 + g, 1)) for g in range(group)]
                run_blocks(bi, dil, span, blocks, True)
                return carry
            lax.fori_loop(0, nb // group, body, 0)

    tile = 256
    lane_t = lax.broadcasted_iota(jnp.int32, (tile, LANES), 1)
    head0_t = lane_t < HEAD_DIM

    def mix(i, carry):
        rows = pl.ds(pl.multiple_of(i * tile, tile), tile)
        l0, l1, l2 = lse_ref[0, rows, :], lse_ref[1, rows, :], lse_ref[2, rows, :]
        mx = jnp.maximum(jnp.maximum(l0, l1), l2)
        e0, e1, e2 = jnp.exp(l0 - mx), jnp.exp(l1 - mx), jnp.exp(l2 - mx)
        den = e0 + e1 + e2
        o = (e0 / den) * ob_ref[0, rows, :] + (e1 / den) * ob_ref[1, rows, :] + (e2 / den) * ob_ref[2, rows, :]
        o2 = o * o
        s0 = jnp.sum(jnp.where(head0_t, o2, 0.0), axis=-1, keepdims=True)
        s1 = jnp.sum(jnp.where(head0_t, 0.0, o2), axis=-1, keepdims=True)
        ms = jnp.where(head0_t, s0, s1) * (1.0 / HEAD_DIM)
        o_ref[rows, :] = o * lax.rsqrt(ms + NORM_EPS) * g_ref[...]
        return carry

    lax.fori_loop(0, s_len // tile, mix, 0)


def _attention(proj, out_g, q_col0, width):
    b, s, _ = proj.shape
    n_pairs = width // LANES
    qb = q_col0 // LANES
    spec = lambda off: pl.BlockSpec((None, s, LANES), lambda bi, pi: (bi, 0, off + pi))
    return pl.pallas_call(
        _attn_kernel,
        grid=(b, n_pairs),
        in_specs=[spec(qb), spec(qb + n_pairs), spec(qb + 2 * n_pairs),
                  pl.BlockSpec((1, LANES), lambda bi, pi: (0, pi))],
        out_specs=pl.BlockSpec((None, s, LANES), lambda bi, pi: (bi, 0, pi)),
        out_shape=jax.ShapeDtypeStruct((b, s, width), F32),
        scratch_shapes=[
            pltpu.VMEM((len(DILATED_PAIRS), s, LANES), F32),
            pltpu.VMEM((len(DILATED_PAIRS), s, LANES), F32),
        ],
        compiler_params=pltpu.CompilerParams(
            dimension_semantics=("arbitrary", "arbitrary"), vmem_limit_bytes=VMEM_LIMIT_BYTES),
        name="dilated_attn",
    )(proj, proj, proj, out_g)


def _ffn_kernel(x_ref, ya_ref, yb_ref, wo_ref, gn_ref, wg_ref, wu_ref, wd_ref, gf_ref, o_ref, *, final):
    wa = ya_ref.shape[-1]
    x1 = (x_ref[...]
          + jnp.dot(ya_ref[...].astype(BF16), wo_ref[0:wa, :], preferred_element_type=F32)
          + jnp.dot(yb_ref[...].astype(BF16), wo_ref[wa:, :], preferred_element_type=F32))
    h = _rmsnorm(x1, gn_ref[...]).astype(BF16)
    gate = jnp.dot(h, wg_ref[...], preferred_element_type=F32)
    up = jnp.dot(h, wu_ref[...], preferred_element_type=F32)
    act = (gate * _sigmoid(gate)) * up
    x2 = x1 + jnp.dot(act.astype(BF16), wd_ref[...], preferred_element_type=F32)
    o_ref[...] = _rmsnorm(x2, gf_ref[...]) if final else x2


def _ffn(x2d, ya, yb, wo, gn, wg, wu, wd, gf, tm, final):
    t, d = x2d.shape
    wa, wb = ya.shape[-1], yb.shape[-1]
    dff = wg.shape[-1]
    const = lambda m, n: pl.BlockSpec((m, n), lambda i: (0, 0), pipeline_mode=pl.Buffered(1))
    return pl.pallas_call(
        functools.partial(_ffn_kernel, final=final),
        grid=(t // tm,),
        in_specs=[
            pl.BlockSpec((tm, d), lambda i: (i, 0)),
            pl.BlockSpec((tm, wa), lambda i: (i, 0)),
            pl.BlockSpec((tm, wb), lambda i: (i, 0)),
            const(wa + wb, d), const(1, d), const(d, dff), const(d, dff), const(dff, d), const(1, d),
        ],
        out_specs=pl.BlockSpec((tm, d), lambda i: (i, 0)),
        out_shape=jax.ShapeDtypeStruct((t, d), F32),
        compiler_params=pltpu.CompilerParams(
            dimension_semantics=("arbitrary",), vmem_limit_bytes=VMEM_LIMIT_BYTES),
        name="outproj_ffn",
    )(x2d, ya, yb, wo, gn, wg, wu, wd, gf)


def kernel(x, mix_norm_g, w_in, mu_shift, decay_w0, decay_w2, iclr_a0, iclr_a2, gate_g2, k_k, k_a, r_k,
           ln_x_w, ln_x_b, attn_out_g, w_out, ffn_norm_g, w_gate, w_up, w_down, final_norm_g):
    b, s, d = x.shape
    depth = w_in.shape[0]
    w = decay_w0.shape[-1]
    shift_cols = mu_shift.shape[-1]
    attn_w = attn_out_g.shape[-1]
    assert shift_cols == 3 * w + DECAY_LORA + AAA_LORA + GATE_LORA
    assert DECAY_LORA + AAA_LORA == LANES and shift_cols % LANES == 0
    tm = 256
    x2d = x.reshape(b * s, d)
    for i in range(depth):
        proj = _inproj(x2d, mix_norm_g[i][None], w_in[i].astype(BF16), tm).reshape(b, s, -1)
        zeros = jnp.zeros((LANES - DECAY_LORA, w), F32)
        w2p = jnp.concatenate([decay_w2[i], zeros], axis=0)
        a2p = jnp.concatenate([zeros, iclr_a2[i]], axis=0)
        y_a = _rwkv(proj, mu_shift[i][None], decay_w0[i][None], w2p, iclr_a0[i][None], a2p, gate_g2[i],
                    k_k[i][None], k_a[i][None], r_k[i].reshape(1, w), ln_x_w[i][None], ln_x_b[i][None],
                    nrows=2 if b % 2 == 0 else 1)
        y_b = _attention(proj, attn_out_g[i][None], shift_cols, attn_w)
        x2d = _ffn(x2d, y_a.reshape(b * s, w), y_b.reshape(b * s, attn_w), w_out[i].astype(BF16),
                   ffn_norm_g[i][None], w_gate[i].astype(BF16), w_up[i].astype(BF16),
                   w_down[i].astype(BF16), final_norm_g[None], tm, final=(i == depth - 1))
    return x2d.reshape(b, s, d)
```

```python
import functools

import jax
import jax.numpy as jnp
from jax import lax
from jax.experimental import pallas as pl
from jax.experimental.pallas import tpu as pltpu

HEAD_DIM = 64
LANES = 128
DECAY_LORA = 64
AAA_LORA = 64
GATE_LORA = 128
DILATED_PAIRS = ((128, 1), (512, 4), (2048, 16))
ATTN_BLOCK = 128
NORM_EPS = 1e-6
GN_EPS = 64e-5
CHUNK = 64
DECAY_SCALE = 0.6065306597126334
VMEM_LIMIT_BYTES = 56 * 1024 * 1024

F32 = jnp.float32
BF16 = jnp.bfloat16


def _dot(a, b, dims=((1,), (0,))):
    return lax.dot_general(a.astype(BF16), b.astype(BF16), (dims, ((), ())),
                           preferred_element_type=F32)


_NT = ((1,), (1,))
_TN = ((0,), (0,))


def _rmsnorm(x, g):
    return x * lax.rsqrt(jnp.mean(x * x, axis=-1, keepdims=True) + NORM_EPS) * g


def _sigmoid(x):
    return 1.0 / (1.0 + jnp.exp(-x))


def _inproj_kernel(x_ref, g_ref, w_ref, mu_ref, o_ref, carry_ref, *, tiles_per_seq):
    tm = x_ref.shape[0]
    sc = mu_ref.shape[-1]
    @pl.when(pl.program_id(0) % tiles_per_seq == 0)
    def _():
        carry_ref[...] = jnp.zeros_like(carry_ref)

    h = _rmsnorm(x_ref[...], g_ref[...]).astype(BF16)
    p = jnp.dot(h, w_ref[:, :sc], preferred_element_type=F32)
    first = lax.broadcasted_iota(jnp.int32, (tm, sc), 0) == 0
    prev = jnp.where(first, carry_ref[...], pltpu.roll(p, 1, axis=0))
    carry_ref[...] = p[tm - 1:tm, :]
    o_ref[:, :sc] = p + (prev - p) * mu_ref[...]
    o_ref[:, sc:] = jnp.dot(h, w_ref[:, sc:], preferred_element_type=F32)


def _inproj(x2d, g, w_bf16, mu, tm, seq_len):
    t, d = x2d.shape
    n = w_bf16.shape[1]
    sc = mu.shape[-1]
    assert seq_len % tm == 0 and sc % LANES == 0
    return pl.pallas_call(
        functools.partial(_inproj_kernel, tiles_per_seq=seq_len // tm),
        grid=(t // tm,),
        in_specs=[
            pl.BlockSpec((tm, d), lambda i: (i, 0)),
            pl.BlockSpec((1, d), lambda i: (0, 0)),
            pl.BlockSpec((d, n), lambda i: (0, 0)),
            pl.BlockSpec((1, sc), lambda i: (0, 0)),
        ],
        out_specs=pl.BlockSpec((tm, n), lambda i: (i, 0)),
        out_shape=jax.ShapeDtypeStruct((t, n), F32),
        scratch_shapes=[pltpu.VMEM((1, sc), F32)],
        compiler_params=pltpu.CompilerParams(
            dimension_semantics=("arbitrary",), vmem_limit_bytes=VMEM_LIMIT_BYTES),
        name="inproj",
    )(x2d, g, w_bf16, mu)


def _unit_lower_inverses(n_mats):
    c = n_mats[0].shape[0]
    ri = lax.broadcasted_iota(jnp.int32, (c, c), 0)
    ci = lax.broadcasted_iota(jnp.int32, (c, c), 1)
    same16 = (ri >> 4) == (ci >> 4)
    same32 = (ri >> 5) == (ci >> 5)
    eye = jnp.where(ri == ci, 1.0, 0.0).astype(F32)
    p = [jnp.where(same16, n, 0.0) for n in n_mats]
    t = [eye + n for n in p]
    for _ in range(3):
        p = [_dot(x, x) for x in p]
        t = [x + _dot(x, y) for x, y in zip(t, p)]
    for off in ([jnp.where(same32 & jnp.logical_not(same16), n, 0.0) for n in n_mats],
                [jnp.where(same32, 0.0, n) for n in n_mats]):
        q = [_dot(x, n) for x, n in zip(t, off)]
        t = [x + _dot(y, x) for x, y in zip(t, q)]
    return t


def _rwkv_kernel(p_ref, w0_ref, w2_ref, a0_ref, a2_ref, g2_ref, kk_ref, ka_ref, rk_ref,
                 lnw_ref, lnb_ref, o_ref, state_ref):
    c = CHUNK
    nrows = p_ref.shape[0]
    w = w0_ref.shape[-1]
    n_pairs = w // LANES
    rc = nrows * c

    @pl.when(pl.program_id(1) == 0)
    def _():
        state_ref[...] = jnp.zeros_like(state_ref)

    cols = lambda lo, hi: jnp.concatenate([p_ref[b, :, lo:hi] for b in range(nrows)], axis=0)
    r = cols(0, w)
    k = cols(w, 2 * w)
    v = cols(2 * w, 3 * w)
    xwa = cols(3 * w, 3 * w + LANES)
    xg = cols(3 * w + LANES, 3 * w + 2 * LANES)

    z = w0_ref[...] + _dot(jnp.tanh(xwa), w2_ref[...])
    ld = -DECAY_SCALE * _sigmoid(z)
    a = _sigmoid(a0_ref[...] + _dot(xwa, a2_ref[...]))
    g = _dot(_sigmoid(xg), g2_ref[...])

    first_head = lambda n: lax.broadcasted_iota(jnp.int32, (n, LANES), 1) < HEAD_DIM
    head0 = first_head(c)

    def head_sum(x):
        h0 = first_head(x.shape[0])
        s0 = jnp.sum(jnp.where(h0, x, 0.0), axis=-1, keepdims=True)
        s1 = jnp.sum(jnp.where(h0, 0.0, x), axis=-1, keepdims=True)
        return jnp.where(h0, s0, s1)

    ti = lax.broadcasted_iota(jnp.int32, (rc, rc), 0)
    si = lax.broadcasted_iota(jnp.int32, (rc, rc), 1)
    assert c == 64
    tri = jnp.where((si <= ti) & ((si >> 6) == (ti >> 6)), 1.0, 0.0).astype(BF16)
    ld_hi = ld.astype(BF16)
    ld_r1 = ld - ld_hi.astype(F32)
    ld_mid = ld_r1.astype(BF16)
    ld_lo = (ld_r1 - ld_mid.astype(F32)).astype(BF16)
    l_incl = _dot(tri, ld_hi) + _dot(tri, ld_mid) + _dot(tri, ld_lo)
    e_incl = jnp.exp(l_incl)
    e_excl = jnp.exp(l_incl - ld)
    e_inv = jnp.exp(-l_incl)

    kk_all = k * kk_ref[...]
    k2_all = k * (1.0 + (a - 1.0) * ka_ref[...])
    rk_all = r * k2_all * rk_ref[...]
    at_all, bt_all = [], []
    for j in range(n_pairs):
        sl = slice(j * LANES, (j + 1) * LANES)
        kk = kk_all[:, sl]
        kk = kk / jnp.maximum(jnp.sqrt(head_sum(kk * kk)), 1e-12)
        at_all.append(-kk * e_excl[:, sl])
        bt_all.append(kk * a[:, sl] * e_inv[:, sl])
    rt_all = r * e_incl
    kt_all = k2_all * e_inv

    def stack(x):
        return jnp.concatenate([jnp.where(head0, x, 0.0), jnp.where(head0, 0.0, x)], axis=0)

    t4 = lax.broadcasted_iota(jnp.int32, (4 * c, 4 * c), 0) & (c - 1)
    s4 = lax.broadcasted_iota(jnp.int32, (4 * c, 4 * c), 1) & (c - 1)
    is_r_row = lax.broadcasted_iota(jnp.int32, (4 * c, 4 * c), 0) >= 2 * c
    causal = (s4 < t4) | (is_r_row & (s4 == t4))

    units = [(b, j) for b in range(nrows) for j in range(n_pairs)]
    rows = lambda b: slice(b * c, (b + 1) * c)
    lanes = lambda j: slice(j * LANES, (j + 1) * LANES)
    ar_s = [jnp.concatenate([stack(at_all[j][rows(b)]), stack(rt_all[rows(b), lanes(j)])], axis=0).astype(BF16)
            for b, j in units]
    bk_s = [jnp.concatenate([stack(bt_all[j][rows(b)]), stack(kt_all[rows(b), lanes(j)])], axis=0)
            for b, j in units]
    v_s = [stack(v[rows(b), lanes(j)]) for b, j in units]
    n_u = range(len(units))

    m = [jnp.where(causal, _dot(ar_s[i], bk_s[i], _NT), 0.0) for i in n_u]
    t_inv = _unit_lower_inverses([m[i][:2 * c, :2 * c] for i in n_u])
    akv = [_dot(m[i][:2 * c, 2 * c:], v_s[i]) for i in n_u]
    hs = [state_ref[i] for i in n_u]
    arh = [_dot(ar_s[i], hs[i], _NT) for i in n_u]
    u = [_dot(t_inv[i], arh[i][:2 * c] + akv[i]) for i in n_u]
    uv = [jnp.concatenate([u[i], v_s[i]], axis=0) for i in n_u]
    for i, (b, j) in enumerate(units):
        wc = e_incl[b * c + c - 1:(b + 1) * c, lanes(j)]
        state_ref[i] = hs[i] * wc + _dot(uv[i], bk_s[i] * wc, _TN)
    y_s = [arh[i][2 * c:] + _dot(m[i][2 * c:, :], uv[i]) for i in n_u]

    for i, (b, j) in enumerate(units):
        sl = lanes(j)
        y = y_s[i][:c] + y_s[i][c:]
        mean = head_sum(y) * (1.0 / HEAD_DIM)
        yc = y - mean
        var = head_sum(yc * yc) * (1.0 / HEAD_DIM)
        yn = yc * lax.rsqrt(var + GN_EPS) * lnw_ref[:, sl] + lnb_ref[:, sl]
        bonus = head_sum(rk_all[rows(b), sl]) * v[rows(b), sl]
        o_ref[b, :, sl] = (yn + bonus) * g[rows(b), sl]


def _rwkv(proj, w0, w2p, a0, a2p, g2, k_k, k_a, r_k, ln_w, ln_b, nrows):
    b, s, _ = proj.shape
    w = w0.shape[-1]
    shift_cols = 3 * w + 2 * LANES
    c = CHUNK
    row = lambda n: pl.BlockSpec((1, n), lambda bi, ci: (0, 0))
    mat = lambda m, n: pl.BlockSpec((m, n), lambda bi, ci: (0, 0))
    return pl.pallas_call(
        _rwkv_kernel,
        grid=(b // nrows, s // c),
        in_specs=[
            pl.BlockSpec((nrows, c, shift_cols), lambda bi, ci: (bi, ci, 0)),
            row(w), mat(LANES, w), row(w), mat(LANES, w), mat(GATE_LORA, w),
            row(w), row(w), row(w), row(w), row(w),
        ],
        out_specs=pl.BlockSpec((nrows, c, w), lambda bi, ci: (bi, ci, 0)),
        out_shape=jax.ShapeDtypeStruct((b, s, w), F32),
        scratch_shapes=[pltpu.VMEM((nrows * (w // LANES), LANES, LANES), F32)],
        compiler_params=pltpu.CompilerParams(
            dimension_semantics=("arbitrary", "arbitrary"), vmem_limit_bytes=VMEM_LIMIT_BYTES),
        name="rwkv7",
    )(proj, w0, w2p, a0, a2p, g2, k_k, k_a, r_k, ln_w, ln_b)


def _attn_kernel(q_ref, k_ref, v_ref, g_ref, o_ref, ob_ref, lse_ref):
    s_len = q_ref.shape[0]
    blk = ATTN_BLOCK
    scale = HEAD_DIM ** -0.5
    lane = lax.broadcasted_iota(jnp.int32, (blk, LANES), 1)
    head0 = lane < HEAD_DIM

    def stack(x):
        return jnp.concatenate([jnp.where(head0, x, 0.0), jnp.where(head0, 0.0, x)], axis=0)

    def run_blocks(bi, dil, span, blocks, has_prev):
        ds = lambda s: pl.ds(s, blk, stride=dil) if dil > 1 else pl.ds(s, blk)
        nk = 2 * blk if has_prev else blk
        qi = lax.broadcasted_iota(jnp.int32, (2 * blk, nk), 0) & (blk - 1)
        kj = lax.broadcasted_iota(jnp.int32, (2 * blk, nk), 1)
        rel = qi - kj + (blk if has_prev else 0)
        in_window = (rel >= 0) & (rel <= span)
        rows = [ds(start) for start, _, _ in blocks]
        q_s = [stack(q_ref[r, :] * scale).astype(BF16) for r in rows]
        if has_prev:
            kcat = [jnp.concatenate([k_ref[ds(ps), :], k_ref[r, :]], axis=0).astype(BF16)
                    for r, (_, ps, _) in zip(rows, blocks)]
            vcat = [jnp.concatenate([v_ref[ds(ps), :], v_ref[r, :]], axis=0).astype(BF16)
                    for r, (_, ps, _) in zip(rows, blocks)]
            valid = [in_window & (pv * blk + kj >= blk) for _, _, pv in blocks]
        else:
            kcat = [k_ref[r, :].astype(BF16) for r in rows]
            vcat = [v_ref[r, :].astype(BF16) for r in rows]
            valid = [in_window for _ in blocks]
        sc = [jnp.where(ok, _dot(q, kc, _NT), -jnp.inf) for q, kc, ok in zip(q_s, kcat, valid)]
        mx = [jnp.max(x, axis=-1, keepdims=True) for x in sc]
        pe = [jnp.exp(x - m) for x, m in zip(sc, mx)]
        den = [jnp.sum(x, axis=-1, keepdims=True) for x in pe]
        o_s = [_dot(x, vc) / d for x, vc, d in zip(pe, vcat, den)]
        lse = [m + jnp.log(d) for m, d in zip(mx, den)]
        for r, o, l in zip(rows, o_s, lse):
            ob_ref[bi, r, :] = jnp.where(head0, o[:blk], o[blk:])
            lse_ref[bi, r, :] = jnp.where(head0, l[:blk], l[blk:])

    group = 4
    for bi, (window, dil) in enumerate(DILATED_PAIRS):
        sub_len = s_len // dil
        span = window // dil
        assert sub_len % blk == 0
        nb = sub_len // blk
        if nb == 1:
            assert dil % group == 0

            def body(i, carry, bi=bi, dil=dil, span=span):
                run_blocks(bi, dil, span, [(i * group + g, None, None) for g in range(group)], False)
                return carry
            lax.fori_loop(0, dil // group, body, 0)
        elif nb == group:
            def body(r, carry, bi=bi, dil=dil, span=span):
                blocks = [(n * blk * dil + r, max(n - 1, 0) * blk * dil + r, min(n, 1)) for n in range(group)]
                run_blocks(bi, dil, span, blocks, True)
                return carry
            lax.fori_loop(0, dil, body, 0)
        else:
            assert dil == 1 and nb % group == 0

            def body(i, carry, bi=bi, dil=dil, span=span):
                n0 = i * group
                blocks = [(pl.multiple_of((n0 + g) * blk, blk),
                           pl.multiple_of(jnp.maximum(n0 + g - 1, 0) * blk, blk),
                           jnp.minimum(n0 + g, 1)) for g in range(group)]
                run_blocks(bi, dil, span, blocks, True)
                return carry
            lax.fori_loop(0, nb // group, body, 0)

    tile = 256
    lane_t = lax.broadcasted_iota(jnp.int32, (tile, LANES), 1)
    head0_t = lane_t < HEAD_DIM

    def mix(i, carry):
        rows = pl.ds(pl.multiple_of(i * tile, tile), tile)
        l0, l1, l2 = lse_ref[0, rows, :], lse_ref[1, rows, :], lse_ref[2, rows, :]
        mx = jnp.maximum(jnp.maximum(l0, l1), l2)
        e0, e1, e2 = jnp.exp(l0 - mx), jnp.exp(l1 - mx), jnp.exp(l2 - mx)
        den = e0 + e1 + e2
        o = (e0 / den) * ob_ref[0, rows, :] + (e1 / den) * ob_ref[1, rows, :] + (e2 / den) * ob_ref[2, rows, :]
        o2 = o * o
        s0 = jnp.sum(jnp.where(head0_t, o2, 0.0), axis=-1, keepdims=True)
        s1 = jnp.sum(jnp.where(head0_t, 0.0, o2), axis=-1, keepdims=True)
        ms = jnp.where(head0_t, s0, s1) * (1.0 / HEAD_DIM)
        o_ref[rows, :] = o * lax.rsqrt(ms + NORM_EPS) * g_ref[...]
        return carry

    lax.fori_loop(0, s_len // tile, mix, 0)


def _attention(proj, out_g, q_col0, width):
    b, s, _ = proj.shape
    n_pairs = width // LANES
    qb = q_col0 // LANES
    spec = lambda off: pl.BlockSpec((None, s, LANES), lambda bi, pi: (bi, 0, off + pi))
    return pl.pallas_call(
        _attn_kernel,
        grid=(b, n_pairs),
        in_specs=[spec(qb), spec(qb + n_pairs), spec(qb + 2 * n_pairs),
                  pl.BlockSpec((1, LANES), lambda bi, pi: (0, pi))],
        out_specs=pl.BlockSpec((None, s, LANES), lambda bi, pi: (bi, 0, pi)),
        out_shape=jax.ShapeDtypeStruct((b, s, width), F32),
        scratch_shapes=[
            pltpu.VMEM((len(DILATED_PAIRS), s, LANES), F32),
            pltpu.VMEM((len(DILATED_PAIRS), s, LANES), F32),
        ],
        compiler_params=pltpu.CompilerParams(
            dimension_semantics=("arbitrary", "arbitrary"), vmem_limit_bytes=VMEM_LIMIT_BYTES),
        name="dilated_attn",
    )(proj, proj, proj, out_g)


def _ffn_kernel(x_ref, ya_ref, yb_ref, wo_ref, gn_ref, wg_ref, wu_ref, wd_ref, gf_ref, o_ref, *, final):
    wa = ya_ref.shape[-1]
    x1 = (x_ref[...]
          + jnp.dot(ya_ref[...].astype(BF16), wo_ref[0:wa, :], preferred_element_type=F32)
          + jnp.dot(yb_ref[...].astype(BF16), wo_ref[wa:, :], preferred_element_type=F32))
    h = _rmsnorm(x1, gn_ref[...]).astype(BF16)
    gate = jnp.dot(h, wg_ref[...], preferred_element_type=F32)
    up = jnp.dot(h, wu_ref[...], preferred_element_type=F32)
    act = (gate * _sigmoid(gate)) * up
    x2 = x1 + jnp.dot(act.astype(BF16), wd_ref[...], preferred_element_type=F32)
    o_ref[...] = _rmsnorm(x2, gf_ref[...]) if final else x2


def _ffn(x2d, ya, yb, wo, gn, wg, wu, wd, gf, tm, final):
    t, d = x2d.shape
    wa, wb = ya.shape[-1], yb.shape[-1]
    dff = wg.shape[-1]
    const = lambda m, n: pl.BlockSpec((m, n), lambda i: (0, 0), pipeline_mode=pl.Buffered(1))
    return pl.pallas_call(
        functools.partial(_ffn_kernel, final=final),
        grid=(t // tm,),
        in_specs=[
            pl.BlockSpec((tm, d), lambda i: (i, 0)),
            pl.BlockSpec((tm, wa), lambda i: (i, 0)),
            pl.BlockSpec((tm, wb), lambda i: (i, 0)),
            const(wa + wb, d), const(1, d), const(d, dff), const(d, dff), const(dff, d), const(1, d),
        ],
        out_specs=pl.BlockSpec((tm, d), lambda i: (i, 0)),
        out_shape=jax.ShapeDtypeStruct((t, d), F32),
        compiler_params=pltpu.CompilerParams(
            dimension_semantics=("arbitrary",), vmem_limit_bytes=VMEM_LIMIT_BYTES),
        name="outproj_ffn",
    )(x2d, ya, yb, wo, gn, wg, wu, wd, gf)


def kernel(x, mix_norm_g, w_in, mu_shift, decay_w0, decay_w2, iclr_a0, iclr_a2, gate_g2, k_k, k_a, r_k,
           ln_x_w, ln_x_b, attn_out_g, w_out, ffn_norm_g, w_gate, w_up, w_down, final_norm_g):
    b, s, d = x.shape
    depth = w_in.shape[0]
    w = decay_w0.shape[-1]
    shift_cols = mu_shift.shape[-1]
    attn_w = attn_out_g.shape[-1]
    assert shift_cols == 3 * w + DECAY_LORA + AAA_LORA + GATE_LORA
    assert DECAY_LORA + AAA_LORA == LANES and shift_cols % LANES == 0
    tm = 512
    x2d = x.reshape(b * s, d)
    for i in range(depth):
        proj = _inproj(x2d, mix_norm_g[i][None], w_in[i].astype(BF16), mu_shift[i][None], tm, s).reshape(b, s, -1)
        zeros = jnp.zeros((LANES - DECAY_LORA, w), F32)
        w2p = jnp.concatenate([decay_w2[i], zeros], axis=0)
        a2p = jnp.concatenate([zeros, iclr_a2[i]], axis=0)
        y_a = _rwkv(proj, decay_w0[i][None], w2p, iclr_a0[i][None], a2p, gate_g2[i],
                    k_k[i][None], k_a[i][None], r_k[i].reshape(1, w), ln_x_w[i][None], ln_x_b[i][None],
                    nrows=2 if b % 2 == 0 else 1)
        y_b = _attention(proj, attn_out_g[i][None], shift_cols, attn_w)
        x2d = _ffn(x2d, y_a.reshape(b * s, w), y_b.reshape(b * s, attn_w), w_out[i].astype(BF16),
                   ffn_norm_g[i][None], w_gate[i].astype(BF16), w_up[i].astype(BF16),
                   w_down[i].astype(BF16), final_norm_g[None], tm, final=(i == depth - 1))
    return x2d.reshape(b, s, d)
```

```python
import functools

import jax
import jax.numpy as jnp
from jax import lax
from jax.experimental import pallas as pl
from jax.experimental.pallas import tpu as pltpu

HEAD_DIM = 64
LANES = 128
DECAY_LORA = 64
AAA_LORA = 64
GATE_LORA = 128
DILATED_PAIRS = ((128, 1), (512, 4), (2048, 16))
ATTN_BLOCK = 128
NORM_EPS = 1e-6
GN_EPS = 64e-5
CHUNK = 64
DECAY_SCALE = 0.6065306597126334
LOG2_E = 1.4426950408889634
LN_2 = 0.6931471805599453
VMEM_LIMIT_BYTES = 56 * 1024 * 1024

F32 = jnp.float32
BF16 = jnp.bfloat16


def _dot(a, b, dims=((1,), (0,))):
    return lax.dot_general(a.astype(BF16), b.astype(BF16), (dims, ((), ())),
                           preferred_element_type=F32)


_NT = ((1,), (1,))
_TN = ((0,), (0,))


def _rmsnorm(x, g):
    return x * lax.rsqrt(jnp.mean(x * x, axis=-1, keepdims=True) + NORM_EPS) * g


def _sigmoid(x):
    return 1.0 / (1.0 + jnp.exp(-x))


def _inproj_kernel(x_ref, g_ref, w_ref, mu_ref, o_ref, carry_ref, *, tiles_per_seq):
    tm = x_ref.shape[0]
    sc = mu_ref.shape[-1]
    @pl.when(pl.program_id(0) % tiles_per_seq == 0)
    def _():
        carry_ref[...] = jnp.zeros_like(carry_ref)

    h = _rmsnorm(x_ref[...], g_ref[...]).astype(BF16)
    p = jnp.dot(h, w_ref[:, :sc], preferred_element_type=F32)
    first = lax.broadcasted_iota(jnp.int32, (tm, sc), 0) == 0
    prev = jnp.where(first, carry_ref[...], pltpu.roll(p, 1, axis=0))
    carry_ref[...] = p[tm - 1:tm, :]
    o_ref[:, :sc] = p + (prev - p) * mu_ref[...]
    o_ref[:, sc:] = jnp.dot(h, w_ref[:, sc:], preferred_element_type=F32)


def _inproj(x2d, g, w_bf16, mu, tm, seq_len):
    t, d = x2d.shape
    n = w_bf16.shape[1]
    sc = mu.shape[-1]
    assert seq_len % tm == 0 and sc % LANES == 0
    return pl.pallas_call(
        functools.partial(_inproj_kernel, tiles_per_seq=seq_len // tm),
        grid=(t // tm,),
        in_specs=[
            pl.BlockSpec((tm, d), lambda i: (i, 0)),
            pl.BlockSpec((1, d), lambda i: (0, 0)),
            pl.BlockSpec((d, n), lambda i: (0, 0)),
            pl.BlockSpec((1, sc), lambda i: (0, 0)),
        ],
        out_specs=pl.BlockSpec((tm, n), lambda i: (i, 0)),
        out_shape=jax.ShapeDtypeStruct((t, n), F32),
        scratch_shapes=[pltpu.VMEM((1, sc), F32)],
        compiler_params=pltpu.CompilerParams(
            dimension_semantics=("arbitrary",), vmem_limit_bytes=VMEM_LIMIT_BYTES),
        name="inproj",
    )(x2d, g, w_bf16, mu)


def _unit_lower_inverses(n_mats):
    c = n_mats[0].shape[0]
    ri = lax.broadcasted_iota(jnp.int32, (c, c), 0)
    ci = lax.broadcasted_iota(jnp.int32, (c, c), 1)
    same16 = (ri >> 4) == (ci >> 4)
    same32 = (ri >> 5) == (ci >> 5)
    eye = jnp.where(ri == ci, 1.0, 0.0).astype(F32)
    p = [jnp.where(same16, n, 0.0) for n in n_mats]
    t = [eye + n for n in p]
    for _ in range(3):
        p = [_dot(x, x) for x in p]
        t = [x + _dot(x, y) for x, y in zip(t, p)]
    for off in ([jnp.where(same32 & jnp.logical_not(same16), n, 0.0) for n in n_mats],
                [jnp.where(same32, 0.0, n) for n in n_mats]):
        q = [_dot(x, n) for x, n in zip(t, off)]
        t = [x + _dot(y, x) for x, y in zip(t, q)]
    return t


def _rwkv_kernel(p_ref, w0_ref, w2_ref, a0_ref, a2_ref, g2_ref, kk_ref, ka_ref, rk_ref,
                 lnw_ref, lnb_ref, o_ref, state_ref):
    c = CHUNK
    nrows = p_ref.shape[0]
    w = w0_ref.shape[-1]
    n_pairs = w // LANES
    rc = nrows * c

    @pl.when(pl.program_id(1) == 0)
    def _():
        state_ref[...] = jnp.zeros_like(state_ref)

    cols = lambda lo, hi: jnp.concatenate([p_ref[b, :, lo:hi] for b in range(nrows)], axis=0)
    r = cols(0, w)
    k = cols(w, 2 * w)
    v = cols(2 * w, 3 * w)
    xwa = cols(3 * w, 3 * w + LANES)
    xg = cols(3 * w + LANES, 3 * w + 2 * LANES)

    z = w0_ref[...] + _dot(jnp.tanh(xwa), w2_ref[...])
    ld = -DECAY_SCALE * _sigmoid(z)
    a = _sigmoid(a0_ref[...] + _dot(xwa, a2_ref[...]))
    g = _dot(_sigmoid(xg), g2_ref[...])

    first_head = lambda n: lax.broadcasted_iota(jnp.int32, (n, LANES), 1) < HEAD_DIM
    head0 = first_head(c)

    def head_sum(x):
        h0 = first_head(x.shape[0])
        s0 = jnp.sum(jnp.where(h0, x, 0.0), axis=-1, keepdims=True)
        s1 = jnp.sum(jnp.where(h0, 0.0, x), axis=-1, keepdims=True)
        return jnp.where(h0, s0, s1)

    ti = lax.broadcasted_iota(jnp.int32, (rc, rc), 0)
    si = lax.broadcasted_iota(jnp.int32, (rc, rc), 1)
    assert c == 64
    tri = jnp.where((si <= ti) & ((si >> 6) == (ti >> 6)), 1.0, 0.0).astype(BF16)
    ld_hi = ld.astype(BF16)
    ld_r1 = ld - ld_hi.astype(F32)
    ld_mid = ld_r1.astype(BF16)
    ld_lo = (ld_r1 - ld_mid.astype(F32)).astype(BF16)
    l_incl = _dot(tri, ld_hi) + _dot(tri, ld_mid) + _dot(tri, ld_lo)
    e_incl = jnp.exp(l_incl)
    e_excl = jnp.exp(l_incl - ld)
    e_inv = jnp.exp(-l_incl)

    kk_all = k * kk_ref[...]
    k2_all = k * (1.0 + (a - 1.0) * ka_ref[...])
    rk_all = r * k2_all * rk_ref[...]
    at_all, bt_all = [], []
    for j in range(n_pairs):
        sl = slice(j * LANES, (j + 1) * LANES)
        kk = kk_all[:, sl]
        kk = kk / jnp.maximum(jnp.sqrt(head_sum(kk * kk)), 1e-12)
        at_all.append(-kk * e_excl[:, sl])
        bt_all.append(kk * a[:, sl] * e_inv[:, sl])
    rt_all = r * e_incl
    kt_all = k2_all * e_inv

    def stack(x):
        return jnp.concatenate([jnp.where(head0, x, 0.0), jnp.where(head0, 0.0, x)], axis=0)

    t4 = lax.broadcasted_iota(jnp.int32, (4 * c, 4 * c), 0) & (c - 1)
    s4 = lax.broadcasted_iota(jnp.int32, (4 * c, 4 * c), 1) & (c - 1)
    is_r_row = lax.broadcasted_iota(jnp.int32, (4 * c, 4 * c), 0) >= 2 * c
    causal = (s4 < t4) | (is_r_row & (s4 == t4))

    units = [(b, j) for b in range(nrows) for j in range(n_pairs)]
    rows = lambda b: slice(b * c, (b + 1) * c)
    lanes = lambda j: slice(j * LANES, (j + 1) * LANES)
    ar_s = [jnp.concatenate([stack(at_all[j][rows(b)]), stack(rt_all[rows(b), lanes(j)])], axis=0).astype(BF16)
            for b, j in units]
    bk_s = [jnp.concatenate([stack(bt_all[j][rows(b)]), stack(kt_all[rows(b), lanes(j)])], axis=0)
            for b, j in units]
    v_s = [stack(v[rows(b), lanes(j)]) for b, j in units]
    n_u = range(len(units))

    m = [jnp.where(causal, _dot(ar_s[i], bk_s[i], _NT), 0.0) for i in n_u]
    t_inv = _unit_lower_inverses([m[i][:2 * c, :2 * c] for i in n_u])
    akv = [_dot(m[i][:2 * c, 2 * c:], v_s[i]) for i in n_u]
    hs = [state_ref[i] for i in n_u]
    arh = [_dot(ar_s[i], hs[i], _NT) for i in n_u]
    u = [_dot(t_inv[i], arh[i][:2 * c] + akv[i]) for i in n_u]
    uv = [jnp.concatenate([u[i], v_s[i]], axis=0) for i in n_u]
    for i, (b, j) in enumerate(units):
        wc = e_incl[b * c + c - 1:(b + 1) * c, lanes(j)]
        state_ref[i] = hs[i] * wc + _dot(uv[i], bk_s[i] * wc, _TN)
    y_s = [arh[i][2 * c:] + _dot(m[i][2 * c:, :], uv[i]) for i in n_u]

    for i, (b, j) in enumerate(units):
        sl = lanes(j)
        y = y_s[i][:c] + y_s[i][c:]
        mean = head_sum(y) * (1.0 / HEAD_DIM)
        yc = y - mean
        var = head_sum(yc * yc) * (1.0 / HEAD_DIM)
        yn = yc * lax.rsqrt(var + GN_EPS) * lnw_ref[:, sl] + lnb_ref[:, sl]
        bonus = head_sum(rk_all[rows(b), sl]) * v[rows(b), sl]
        o_ref[b, :, sl] = (yn + bonus) * g[rows(b), sl]


def _rwkv(proj, w0, w2p, a0, a2p, g2, k_k, k_a, r_k, ln_w, ln_b, nrows):
    b, s, _ = proj.shape
    w = w0.shape[-1]
    shift_cols = 3 * w + 2 * LANES
    c = CHUNK
    row = lambda n: pl.BlockSpec((1, n), lambda bi, ci: (0, 0))
    mat = lambda m, n: pl.BlockSpec((m, n), lambda bi, ci: (0, 0))
    return pl.pallas_call(
        _rwkv_kernel,
        grid=(b // nrows, s // c),
        in_specs=[
            pl.BlockSpec((nrows, c, shift_cols), lambda bi, ci: (bi, ci, 0)),
            row(w), mat(LANES, w), row(w), mat(LANES, w), mat(GATE_LORA, w),
            row(w), row(w), row(w), row(w), row(w),
        ],
        out_specs=pl.BlockSpec((nrows, c, w), lambda bi, ci: (bi, ci, 0)),
        out_shape=jax.ShapeDtypeStruct((b, s, w), F32),
        scratch_shapes=[pltpu.VMEM((nrows * (w // LANES), LANES, LANES), F32)],
        compiler_params=pltpu.CompilerParams(
            dimension_semantics=("arbitrary", "arbitrary"), vmem_limit_bytes=VMEM_LIMIT_BYTES),
        name="rwkv7",
    )(proj, w0, w2p, a0, a2p, g2, k_k, k_a, r_k, ln_w, ln_b)


def _attn_kernel(q_ref, k_ref, v_ref, g_ref, o_ref, ob_ref, lse_ref):
    s_len = q_ref.shape[0]
    blk = ATTN_BLOCK
    scale = HEAD_DIM ** -0.5
    lane = lax.broadcasted_iota(jnp.int32, (blk, LANES), 1)
    head0 = lane < HEAD_DIM

    def stack(x):
        return jnp.concatenate([jnp.where(head0, x, 0.0), jnp.where(head0, 0.0, x)], axis=0)

    def window_bias(span, has_prev):
        nk = 2 * blk if has_prev else blk
        qi = lax.broadcasted_iota(jnp.int32, (2 * blk, nk), 0) & (blk - 1)
        kj = lax.broadcasted_iota(jnp.int32, (2 * blk, nk), 1)
        rel = qi - kj + (blk if has_prev else 0)
        return jnp.where((rel >= 0) & (rel <= span), 0.0, -jnp.inf).astype(F32)

    def scores_stage(bi, dil, bias, blocks):
        ds = lambda s: pl.ds(s, blk, stride=dil) if dil > 1 else pl.ds(s, blk)
        rows = [ds(start) for start, _ in blocks]
        q_s = [stack(q_ref[r, :] * (scale * LOG2_E)).astype(BF16) for r in rows]
        kcat = [k_ref[r, :].astype(BF16) if ps is None else
                jnp.concatenate([k_ref[ds(ps), :], k_ref[r, :]], axis=0).astype(BF16)
                for r, (_, ps) in zip(rows, blocks)]
        vcat = [v_ref[r, :].astype(BF16) if ps is None else
                jnp.concatenate([v_ref[ds(ps), :], v_ref[r, :]], axis=0).astype(BF16)
                for r, (_, ps) in zip(rows, blocks)]
        sc = [_dot(q, kc, _NT) + bias[ps is not None]
              for q, kc, (_, ps) in zip(q_s, kcat, blocks)]
        return bi, rows, sc, vcat

    def softmax_stage(bi, rows, sc, vcat):
        mx = [jnp.max(x, axis=-1, keepdims=True) for x in sc]
        pe = [jnp.exp2(x - m) for x, m in zip(sc, mx)]
        den = [jnp.sum(x, axis=-1, keepdims=True) for x in pe]
        o_s = [_dot(x, vc) for x, vc in zip(pe, vcat)]
        for r, o, m, d in zip(rows, o_s, mx, den):
            d = jnp.where(head0, d[:blk], d[blk:])
            ob_ref[bi, r, :] = jnp.where(head0, o[:blk], o[blk:]) / d
            lse_ref[bi, r, :] = jnp.where(head0, m[:blk], m[blk:]) * LN_2 + jnp.log(d)

    group = 2
    groups = []
    for bi, (window, dil) in enumerate(DILATED_PAIRS):
        sub_len = s_len // dil
        span = window // dil
        assert sub_len % blk == 0
        nb = sub_len // blk
        bias = {False: window_bias(span, False), True: window_bias(span, True) if nb > 1 else None}
        blocks = [(n * blk * dil + r, (n - 1) * blk * dil + r if n else None)
                  for r in range(dil) for n in range(nb)]
        assert len(blocks) % group == 0
        groups += [(bi, dil, bias, blocks[i:i + group]) for i in range(0, len(blocks), group)]
    pending = None
    for g in groups:
        nxt = scores_stage(*g)
        if pending is not None:
            softmax_stage(*pending)
        pending = nxt
    softmax_stage(*pending)

    tile = 256
    lane_t = lax.broadcasted_iota(jnp.int32, (tile, LANES), 1)
    head0_t = lane_t < HEAD_DIM

    def mix(i, carry):
        rows = pl.ds(pl.multiple_of(i * tile, tile), tile)
        l0, l1, l2 = lse_ref[0, rows, :], lse_ref[1, rows, :], lse_ref[2, rows, :]
        mx = jnp.maximum(jnp.maximum(l0, l1), l2)
        e0, e1, e2 = jnp.exp(l0 - mx), jnp.exp(l1 - mx), jnp.exp(l2 - mx)
        den = e0 + e1 + e2
        o = (e0 / den) * ob_ref[0, rows, :] + (e1 / den) * ob_ref[1, rows, :] + (e2 / den) * ob_ref[2, rows, :]
        o2 = o * o
        s0 = jnp.sum(jnp.where(head0_t, o2, 0.0), axis=-1, keepdims=True)
        s1 = jnp.sum(jnp.where(head0_t, 0.0, o2), axis=-1, keepdims=True)
        ms = jnp.where(head0_t, s0, s1) * (1.0 / HEAD_DIM)
        o_ref[rows, :] = o * lax.rsqrt(ms + NORM_EPS) * g_ref[...]
        return carry

    lax.fori_loop(0, s_len // tile, mix, 0)


def _attention(proj, out_g, q_col0, width):
    b, s, _ = proj.shape
    n_pairs = width // LANES
    qb = q_col0 // LANES
    spec = lambda off: pl.BlockSpec((None, s, LANES), lambda bi, pi: (bi, 0, off + pi))
    return pl.pallas_call(
        _attn_kernel,
        grid=(b, n_pairs),
        in_specs=[spec(qb), spec(qb + n_pairs), spec(qb + 2 * n_pairs),
                  pl.BlockSpec((1, LANES), lambda bi, pi: (0, pi))],
        out_specs=pl.BlockSpec((None, s, LANES), lambda bi, pi: (bi, 0, pi)),
        out_shape=jax.ShapeDtypeStruct((b, s, width), F32),
        scratch_shapes=[
            pltpu.VMEM((len(DILATED_PAIRS), s, LANES), F32),
            pltpu.VMEM((len(DILATED_PAIRS), s, LANES), F32),
        ],
        compiler_params=pltpu.CompilerParams(
            dimension_semantics=("arbitrary", "arbitrary"), vmem_limit_bytes=VMEM_LIMIT_BYTES),
        name="dilated_attn",
    )(proj, proj, proj, out_g)


def _ffn_kernel(x_ref, ya_ref, yb_ref, wo_ref, gn_ref, wg_ref, wu_ref, wd_ref, gf_ref, o_ref, *, final):
    wa = ya_ref.shape[-1]
    x1 = (x_ref[...]
          + jnp.dot(ya_ref[...].astype(BF16), wo_ref[0:wa, :], preferred_element_type=F32)
          + jnp.dot(yb_ref[...].astype(BF16), wo_ref[wa:, :], preferred_element_type=F32))
    h = _rmsnorm(x1, gn_ref[...]).astype(BF16)
    gate = jnp.dot(h, wg_ref[...], preferred_element_type=F32)
    up = jnp.dot(h, wu_ref[...], preferred_element_type=F32)
    act = (gate * _sigmoid(gate)) * up
    x2 = x1 + jnp.dot(act.astype(BF16), wd_ref[...], preferred_element_type=F32)
    o_ref[...] = _rmsnorm(x2, gf_ref[...]) if final else x2


def _ffn(x2d, ya, yb, wo, gn, wg, wu, wd, gf, tm, final):
    t, d = x2d.shape
    wa, wb = ya.shape[-1], yb.shape[-1]
    dff = wg.shape[-1]
    const = lambda m, n: pl.BlockSpec((m, n), lambda i: (0, 0), pipeline_mode=pl.Buffered(1))
    return pl.pallas_call(
        functools.partial(_ffn_kernel, final=final),
        grid=(t // tm,),
        in_specs=[
            pl.BlockSpec((tm, d), lambda i: (i, 0)),
            pl.BlockSpec((tm, wa), lambda i: (i, 0)),
            pl.BlockSpec((tm, wb), lambda i: (i, 0)),
            const(wa + wb, d), const(1, d), const(d, dff), const(d, dff), const(dff, d), const(1, d),
        ],
        out_specs=pl.BlockSpec((tm, d), lambda i: (i, 0)),
        out_shape=jax.ShapeDtypeStruct((t, d), F32),
        compiler_params=pltpu.CompilerParams(
            dimension_semantics=("arbitrary",), vmem_limit_bytes=VMEM_LIMIT_BYTES),
        name="outproj_ffn",
    )(x2d, ya, yb, wo, gn, wg, wu, wd, gf)


def kernel(x, mix_norm_g, w_in, mu_shift, decay_w0, decay_w2, iclr_a0, iclr_a2, gate_g2, k_k, k_a, r_k,
           ln_x_w, ln_x_b, attn_out_g, w_out, ffn_norm_g, w_gate, w_up, w_down, final_norm_g):
    b, s, d = x.shape
    depth = w_in.shape[0]
    w = decay_w0.shape[-1]
    shift_cols = mu_shift.shape[-1]
    attn_w = attn_out_g.shape[-1]
    assert shift_cols == 3 * w + DECAY_LORA + AAA_LORA + GATE_LORA
    assert DECAY_LORA + AAA_LORA == LANES and shift_cols % LANES == 0
    tm = 512
    x2d = x.reshape(b * s, d)
    for i in range(depth):
        proj = _inproj(x2d, mix_norm_g[i][None], w_in[i].astype(BF16), mu_shift[i][None], tm, s).reshape(b, s, -1)
        zeros = jnp.zeros((LANES - DECAY_LORA, w), F32)
        w2p = jnp.concatenate([decay_w2[i], zeros], axis=0)
        a2p = jnp.concatenate([zeros, iclr_a2[i]], axis=0)
        y_a = _rwkv(proj, decay_w0[i][None], w2p, iclr_a0[i][None], a2p, gate_g2[i],
                    k_k[i][None], k_a[i][None], r_k[i].reshape(1, w), ln_x_w[i][None], ln_x_b[i][None],
                    nrows=4 if b % 4 == 0 else 1)
        y_b = _attention(proj, attn_out_g[i][None], shift_cols, attn_w)
        x2d = _ffn(x2d, y_a.reshape(b * s, w), y_b.reshape(b * s, attn_w), w_out[i].astype(BF16),
                   ffn_norm_g[i][None], w_gate[i].astype(BF16), w_up[i].astype(BF16),
                   w_down[i].astype(BF16), final_norm_g[None], tm, final=(i == depth - 1))
    return x2d.reshape(b, s, d)
```

```python
import functools

import jax
import jax.numpy as jnp
from jax import lax
from jax.experimental import pallas as pl
from jax.experimental.pallas import tpu as pltpu

HEAD_DIM = 64
LANES = 128
DECAY_LORA = 64
AAA_LORA = 64
GATE_LORA = 128
DILATED_PAIRS = ((128, 1), (512, 4), (2048, 16))
ATTN_BLOCK = 128
NORM_EPS = 1e-6
GN_EPS = 64e-5
CHUNK = 64
DECAY_SCALE = 0.6065306597126334
LOG2_E = 1.4426950408889634
LN_2 = 0.6931471805599453
VMEM_LIMIT_BYTES = 56 * 1024 * 1024

F32 = jnp.float32
BF16 = jnp.bfloat16


def _dot(a, b, dims=((1,), (0,))):
    return lax.dot_general(a.astype(BF16), b.astype(BF16), (dims, ((), ())),
                           preferred_element_type=F32)


_NT = ((1,), (1,))
_TN = ((0,), (0,))


def _rmsnorm(x, g):
    return x * lax.rsqrt(jnp.mean(x * x, axis=-1, keepdims=True) + NORM_EPS) * g


def _sigmoid(x):
    return 1.0 / (1.0 + jnp.exp(-x))


def _inproj_kernel(x_ref, g_ref, w_ref, mu_ref, o_ref, carry_ref, *, tiles_per_seq):
    tm = x_ref.shape[0]
    sc = mu_ref.shape[-1]
    @pl.when(pl.program_id(0) % tiles_per_seq == 0)
    def _():
        carry_ref[...] = jnp.zeros_like(carry_ref)

    h = _rmsnorm(x_ref[...], g_ref[...]).astype(BF16)
    p = jnp.dot(h, w_ref[:, :sc], preferred_element_type=F32)
    first = lax.broadcasted_iota(jnp.int32, (tm, sc), 0) == 0
    prev = jnp.where(first, carry_ref[...], pltpu.roll(p, 1, axis=0))
    carry_ref[...] = p[tm - 1:tm, :]
    o_ref[:, :sc] = p + (prev - p) * mu_ref[...]
    o_ref[:, sc:] = jnp.dot(h, w_ref[:, sc:], preferred_element_type=F32)


def _inproj(x2d, g, w_bf16, mu, tm, seq_len):
    t, d = x2d.shape
    n = w_bf16.shape[1]
    sc = mu.shape[-1]
    assert seq_len % tm == 0 and sc % LANES == 0
    return pl.pallas_call(
        functools.partial(_inproj_kernel, tiles_per_seq=seq_len // tm),
        grid=(t // tm,),
        in_specs=[
            pl.BlockSpec((tm, d), lambda i: (i, 0)),
            pl.BlockSpec((1, d), lambda i: (0, 0)),
            pl.BlockSpec((d, n), lambda i: (0, 0)),
            pl.BlockSpec((1, sc), lambda i: (0, 0)),
        ],
        out_specs=pl.BlockSpec((tm, n), lambda i: (i, 0)),
        out_shape=jax.ShapeDtypeStruct((t, n), F32),
        scratch_shapes=[pltpu.VMEM((1, sc), F32)],
        compiler_params=pltpu.CompilerParams(
            dimension_semantics=("arbitrary",), vmem_limit_bytes=VMEM_LIMIT_BYTES),
        name="inproj",
    )(x2d, g, w_bf16, mu)


def _unit_lower_inverses(n_mats):
    c = n_mats[0].shape[0]
    ri = lax.broadcasted_iota(jnp.int32, (c, c), 0)
    ci = lax.broadcasted_iota(jnp.int32, (c, c), 1)
    same16 = (ri >> 4) == (ci >> 4)
    same32 = (ri >> 5) == (ci >> 5)
    eye = jnp.where(ri == ci, 1.0, 0.0).astype(F32)
    p = [jnp.where(same16, n, 0.0) for n in n_mats]
    t = [eye + n for n in p]
    for _ in range(3):
        p = [_dot(x, x) for x in p]
        t = [x + _dot(x, y) for x, y in zip(t, p)]
    for off in ([jnp.where(same32 & jnp.logical_not(same16), n, 0.0) for n in n_mats],
                [jnp.where(same32, 0.0, n) for n in n_mats]):
        q = [_dot(x, n) for x, n in zip(t, off)]
        t = [x + _dot(y, x) for x, y in zip(t, q)]
    return t


def _rwkv_kernel(p_ref, w0_ref, w2_ref, a0_ref, a2_ref, g2_ref, kk_ref, ka_ref, rk_ref,
                 lnw_ref, lnb_ref, o_ref, state_ref):
    c = CHUNK
    nrows = p_ref.shape[0]
    nchunks = p_ref.shape[1] // c
    w = w0_ref.shape[-1]
    n_pairs = w // LANES
    rc = nrows * c
    assert c == 64

    @pl.when(pl.program_id(1) == 0)
    def _():
        state_ref[...] = jnp.zeros_like(state_ref)

    first_head = lambda n: lax.broadcasted_iota(jnp.int32, (n, LANES), 1) < HEAD_DIM
    head0 = first_head(c)

    def head_sum(x):
        h0 = first_head(x.shape[0])
        s0 = jnp.sum(jnp.where(h0, x, 0.0), axis=-1, keepdims=True)
        s1 = jnp.sum(jnp.where(h0, 0.0, x), axis=-1, keepdims=True)
        return jnp.where(h0, s0, s1)

    def stack(x):
        return jnp.concatenate([jnp.where(head0, x, 0.0), jnp.where(head0, 0.0, x)], axis=0)

    ti = lax.broadcasted_iota(jnp.int32, (rc, rc), 0)
    si = lax.broadcasted_iota(jnp.int32, (rc, rc), 1)
    tri = jnp.where((si <= ti) & ((si >> 6) == (ti >> 6)), 1.0, 0.0).astype(BF16)
    t4 = lax.broadcasted_iota(jnp.int32, (4 * c, 4 * c), 0) & (c - 1)
    s4 = lax.broadcasted_iota(jnp.int32, (4 * c, 4 * c), 1) & (c - 1)
    is_r_row = lax.broadcasted_iota(jnp.int32, (4 * c, 4 * c), 0) >= 2 * c
    causal = (s4 < t4) | (is_r_row & (s4 == t4))

    units = [(b, j) for b in range(nrows) for j in range(n_pairs)]
    n_u = range(len(units))
    rows = lambda b: slice(b * c, (b + 1) * c)
    lanes = lambda j: slice(j * LANES, (j + 1) * LANES)

    def operand_stage(ch):
        cols = lambda lo, hi: jnp.concatenate(
            [p_ref[b, ch * c:(ch + 1) * c, lo:hi] for b in range(nrows)], axis=0)
        r = cols(0, w)
        k = cols(w, 2 * w)
        v = cols(2 * w, 3 * w)
        xwa = cols(3 * w, 3 * w + LANES)
        xg = cols(3 * w + LANES, 3 * w + 2 * LANES)

        z = w0_ref[...] + _dot(jnp.tanh(xwa), w2_ref[...])
        ld = -DECAY_SCALE * _sigmoid(z)
        a = _sigmoid(a0_ref[...] + _dot(xwa, a2_ref[...]))
        g = _dot(_sigmoid(xg), g2_ref[...])

        ld_hi = ld.astype(BF16)
        ld_r1 = ld - ld_hi.astype(F32)
        ld_mid = ld_r1.astype(BF16)
        ld_lo = (ld_r1 - ld_mid.astype(F32)).astype(BF16)
        l_incl = _dot(tri, ld_hi) + _dot(tri, ld_mid) + _dot(tri, ld_lo)
        e_incl = jnp.exp(l_incl)
        e_excl = jnp.exp(l_incl - ld)
        e_inv = jnp.exp(-l_incl)

        kk_all = k * kk_ref[...]
        k2_all = k * (1.0 + (a - 1.0) * ka_ref[...])
        rk_all = r * k2_all * rk_ref[...]
        at_all, bt_all = [], []
        for j in range(n_pairs):
            sl = lanes(j)
            kk = kk_all[:, sl]
            kk = kk * lax.rsqrt(jnp.maximum(head_sum(kk * kk), 1e-24))
            at_all.append(-kk * e_excl[:, sl])
            bt_all.append(kk * a[:, sl] * e_inv[:, sl])
        rt_all = r * e_incl
        kt_all = k2_all * e_inv

        ar_s = [jnp.concatenate([stack(at_all[j][rows(b)]), stack(rt_all[rows(b), lanes(j)])],
                                axis=0).astype(BF16) for b, j in units]
        bk_s = [jnp.concatenate([stack(bt_all[j][rows(b)]), stack(kt_all[rows(b), lanes(j)])],
                                axis=0) for b, j in units]
        v_s = [stack(v[rows(b), lanes(j)]) for b, j in units]
        wc = [e_incl[b * c + c - 1:(b + 1) * c, lanes(j)] for b, j in units]
        bonus = [head_sum(rk_all[rows(b), lanes(j)]) * v[rows(b), lanes(j)] for b, j in units]
        gate = [g[rows(b), lanes(j)] for b, j in units]
        return ar_s, bk_s, v_s, wc, bonus, gate

    def state_free_stage(operands):
        ar_s, bk_s, v_s, wc, bonus, gate = operands
        m = [jnp.where(causal, _dot(ar_s[i], bk_s[i], _NT), 0.0) for i in n_u]
        t_inv = _unit_lower_inverses([m[i][:2 * c, :2 * c] for i in n_u])
        akv = [_dot(m[i][:2 * c, 2 * c:], v_s[i]) for i in n_u]
        m_r = [m[i][2 * c:, :] for i in n_u]
        bkw = [bk_s[i] * wc[i] for i in n_u]
        return ar_s, bkw, v_s, wc, t_inv, akv, m_r, bonus, gate

    def state_stage(ch, hs, ops):
        ar_s, bkw, v_s, wc, t_inv, akv, m_r, bonus, gate = ops
        arh = [_dot(ar_s[i], hs[i], _NT) for i in n_u]
        u = [_dot(t_inv[i], arh[i][:2 * c] + akv[i]) for i in n_u]
        uv = [jnp.concatenate([u[i], v_s[i]], axis=0) for i in n_u]
        hs_new = [hs[i] * wc[i] + _dot(uv[i], bkw[i], _TN) for i in n_u]
        y_s = [arh[i][2 * c:] + _dot(m_r[i], uv[i]) for i in n_u]
        for i, (b, j) in enumerate(units):
            sl = lanes(j)
            y = y_s[i][:c] + y_s[i][c:]
            mean = head_sum(y) * (1.0 / HEAD_DIM)
            yc = y - mean
            var = head_sum(yc * yc) * (1.0 / HEAD_DIM)
            yn = yc * lax.rsqrt(var + GN_EPS) * lnw_ref[:, sl] + lnb_ref[:, sl]
            o_ref[b, ch * c:(ch + 1) * c, sl] = (yn + bonus[i]) * gate[i]
        return hs_new

    operands = [operand_stage(ch) for ch in range(nchunks)]
    ops = [state_free_stage(x) for x in operands]
    hs = [state_ref[i] for i in n_u]
    for ch in range(nchunks):
        hs = state_stage(ch, hs, ops[ch])
    for i in n_u:
        state_ref[i] = hs[i]


def _rwkv(proj, w0, w2p, a0, a2p, g2, k_k, k_a, r_k, ln_w, ln_b, nrows, nchunks):
    b, s, _ = proj.shape
    w = w0.shape[-1]
    shift_cols = 3 * w + 2 * LANES
    c = CHUNK * nchunks
    row = lambda n: pl.BlockSpec((1, n), lambda bi, ci: (0, 0))
    mat = lambda m, n: pl.BlockSpec((m, n), lambda bi, ci: (0, 0))
    return pl.pallas_call(
        _rwkv_kernel,
        grid=(b // nrows, s // c),
        in_specs=[
            pl.BlockSpec((nrows, c, shift_cols), lambda bi, ci: (bi, ci, 0)),
            row(w), mat(LANES, w), row(w), mat(LANES, w), mat(GATE_LORA, w),
            row(w), row(w), row(w), row(w), row(w),
        ],
        out_specs=pl.BlockSpec((nrows, c, w), lambda bi, ci: (bi, ci, 0)),
        out_shape=jax.ShapeDtypeStruct((b, s, w), F32),
        scratch_shapes=[pltpu.VMEM((nrows * (w // LANES), LANES, LANES), F32)],
        compiler_params=pltpu.CompilerParams(
            dimension_semantics=("arbitrary", "arbitrary"), vmem_limit_bytes=VMEM_LIMIT_BYTES),
        name="rwkv7",
    )(proj, w0, w2p, a0, a2p, g2, k_k, k_a, r_k, ln_w, ln_b)


def _attn_kernel(q_ref, k_ref, v_ref, g_ref, o_ref, ob_ref, lse_ref):
    s_len = q_ref.shape[0]
    blk = ATTN_BLOCK
    scale = HEAD_DIM ** -0.5
    lane = lax.broadcasted_iota(jnp.int32, (blk, LANES), 1)
    head0 = lane < HEAD_DIM

    def stack(x):
        return jnp.concatenate([jnp.where(head0, x, 0.0), jnp.where(head0, 0.0, x)], axis=0)

    def window_bias(span, has_prev):
        nk = 2 * blk if has_prev else blk
        qi = lax.broadcasted_iota(jnp.int32, (2 * blk, nk), 0) & (blk - 1)
        kj = lax.broadcasted_iota(jnp.int32, (2 * blk, nk), 1)
        rel = qi - kj + (blk if has_prev else 0)
        return jnp.where((rel >= 0) & (rel <= span), 0.0, -jnp.inf).astype(F32)

    def scores_stage(bi, dil, bias, blocks):
        ds = lambda s: pl.ds(s, blk, stride=dil) if dil > 1 else pl.ds(s, blk)
        rows = [ds(start) for start, _ in blocks]
        q_s = [stack(q_ref[r, :] * (scale * LOG2_E)).astype(BF16) for r in rows]
        kcat = [k_ref[r, :].astype(BF16) if ps is None else
                jnp.concatenate([k_ref[ds(ps), :], k_ref[r, :]], axis=0).astype(BF16)
                for r, (_, ps) in zip(rows, blocks)]
        vcat = [v_ref[r, :].astype(BF16) if ps is None else
                jnp.concatenate([v_ref[ds(ps), :], v_ref[r, :]], axis=0).astype(BF16)
                for r, (_, ps) in zip(rows, blocks)]
        sc = [_dot(q, kc, _NT) + bias[ps is not None]
              for q, kc, (_, ps) in zip(q_s, kcat, blocks)]
        return bi, rows, sc, vcat

    def softmax_stage(bi, rows, sc, vcat):
        mx = [jnp.max(x, axis=-1, keepdims=True) for x in sc]
        pe = [jnp.exp2(x - m) for x, m in zip(sc, mx)]
        den = [jnp.sum(x, axis=-1, keepdims=True) for x in pe]
        o_s = [_dot(x, vc) for x, vc in zip(pe, vcat)]
        res = []
        for o, m, d in zip(o_s, mx, den):
            d = jnp.where(head0, d[:blk], d[blk:])
            res.append((jnp.where(head0, o[:blk], o[blk:]) / d,
                        jnp.where(head0, m[:blk], m[blk:]) + jnp.log(d) * LOG2_E))
        return res

    add = lambda xs: functools.reduce(lambda x, y: x + y, xs)

    def store_stage(slot, rows, res):
        for r, (o, l) in zip(rows, res):
            ob_ref[slot, r, :] = o
            lse_ref[slot, r, :] = l

    def mix_stage(rows, res):
        for r, (o_here, l_here) in zip(rows, res):
            ls = [l_here] + [lse_ref[s, r, :] for s in range(ob_ref.shape[0])]
            os_ = [o_here] + [ob_ref[s, r, :] for s in range(ob_ref.shape[0])]
            mx = functools.reduce(jnp.maximum, ls)
            es = [jnp.exp2(l - mx) for l in ls]
            o = add([e * x for e, x in zip(es, os_)]) / add(es)
            o2 = o * o
            s0 = jnp.sum(jnp.where(head0, o2, 0.0), axis=-1, keepdims=True)
            s1 = jnp.sum(jnp.where(head0, 0.0, o2), axis=-1, keepdims=True)
            ms = jnp.where(head0, s0, s1) * (1.0 / HEAD_DIM)
            o_ref[r, :] = o * lax.rsqrt(ms + NORM_EPS) * g_ref[...]

    group = 2
    groups = []
    order = sorted(range(len(DILATED_PAIRS)), key=lambda b: -DILATED_PAIRS[b][1])
    assert DILATED_PAIRS[order[-1]][1] == 1
    for bi in order:
        window, dil = DILATED_PAIRS[bi]
        sub_len = s_len // dil
        span = window // dil
        assert sub_len % blk == 0
        nb = sub_len // blk
        bias = {False: window_bias(span, False), True: window_bias(span, True) if nb > 1 else None}
        blocks = [(n * blk * dil + r, (n - 1) * blk * dil + r if n else None)
                  for r in range(dil) for n in range(nb)]
        assert len(blocks) % group == 0
        groups += [(bi, dil, bias, blocks[i:i + group]) for i in range(0, len(blocks), group)]

    def finish(bi, rows, sc, vcat):
        res = softmax_stage(bi, rows, sc, vcat)
        if bi == order[-1]:
            mix_stage(rows, res)
        else:
            store_stage(order.index(bi), rows, res)

    pending = None
    for g in groups:
        nxt = scores_stage(*g)
        if pending is not None:
            finish(*pending)
        pending = nxt
    finish(*pending)


def _attention(proj, out_g, q_col0, width):
    b, s, _ = proj.shape
    n_pairs = width // LANES
    qb = q_col0 // LANES
    spec = lambda off: pl.BlockSpec((None, s, LANES), lambda bi, pi: (bi, 0, off + pi))
    return pl.pallas_call(
        _attn_kernel,
        grid=(b, n_pairs),
        in_specs=[spec(qb), spec(qb + n_pairs), spec(qb + 2 * n_pairs),
                  pl.BlockSpec((1, LANES), lambda bi, pi: (0, pi))],
        out_specs=pl.BlockSpec((None, s, LANES), lambda bi, pi: (bi, 0, pi)),
        out_shape=jax.ShapeDtypeStruct((b, s, width), F32),
        scratch_shapes=[
            pltpu.VMEM((len(DILATED_PAIRS) - 1, s, LANES), F32),
            pltpu.VMEM((len(DILATED_PAIRS) - 1, s, LANES), F32),
        ],
        compiler_params=pltpu.CompilerParams(
            dimension_semantics=("arbitrary", "arbitrary"), vmem_limit_bytes=VMEM_LIMIT_BYTES),
        name="dilated_attn",
    )(proj, proj, proj, out_g)


def _ffn_kernel(x_ref, ya_ref, yb_ref, wo_ref, gn_ref, wg_ref, wu_ref, wd_ref, gf_ref, o_ref, *, final):
    wa = ya_ref.shape[-1]
    x1 = (x_ref[...]
          + jnp.dot(ya_ref[...].astype(BF16), wo_ref[0:wa, :], preferred_element_type=F32)
          + jnp.dot(yb_ref[...].astype(BF16), wo_ref[wa:, :], preferred_element_type=F32))
    h = _rmsnorm(x1, gn_ref[...]).astype(BF16)
    gate = jnp.dot(h, wg_ref[...], preferred_element_type=F32)
    up = jnp.dot(h, wu_ref[...], preferred_element_type=F32)
    act = (gate * _sigmoid(gate)) * up
    x2 = x1 + jnp.dot(act.astype(BF16), wd_ref[...], preferred_element_type=F32)
    o_ref[...] = _rmsnorm(x2, gf_ref[...]) if final else x2


def _ffn(x2d, ya, yb, wo, gn, wg, wu, wd, gf, tm, final):
    t, d = x2d.shape
    wa, wb = ya.shape[-1], yb.shape[-1]
    dff = wg.shape[-1]
    const = lambda m, n: pl.BlockSpec((m, n), lambda i: (0, 0), pipeline_mode=pl.Buffered(1))
    return pl.pallas_call(
        functools.partial(_ffn_kernel, final=final),
        grid=(t // tm,),
        in_specs=[
            pl.BlockSpec((tm, d), lambda i: (i, 0)),
            pl.BlockSpec((tm, wa), lambda i: (i, 0)),
            pl.BlockSpec((tm, wb), lambda i: (i, 0)),
            const(wa + wb, d), const(1, d), const(d, dff), const(d, dff), const(dff, d), const(1, d),
        ],
        out_specs=pl.BlockSpec((tm, d), lambda i: (i, 0)),
        out_shape=jax.ShapeDtypeStruct((t, d), F32),
        compiler_params=pltpu.CompilerParams(
            dimension_semantics=("arbitrary",), vmem_limit_bytes=VMEM_LIMIT_BYTES),
        name="outproj_ffn",
    )(x2d, ya, yb, wo, gn, wg, wu, wd, gf)


def kernel(x, mix_norm_g, w_in, mu_shift, decay_w0, decay_w2, iclr_a0, iclr_a2, gate_g2, k_k, k_a, r_k,
           ln_x_w, ln_x_b, attn_out_g, w_out, ffn_norm_g, w_gate, w_up, w_down, final_norm_g):
    b, s, d = x.shape
    depth = w_in.shape[0]
    w = decay_w0.shape[-1]
    shift_cols = mu_shift.shape[-1]
    attn_w = attn_out_g.shape[-1]
    assert shift_cols == 3 * w + DECAY_LORA + AAA_LORA + GATE_LORA
    assert DECAY_LORA + AAA_LORA == LANES and shift_cols % LANES == 0
    tm = 512
    x2d = x.reshape(b * s, d)
    for i in range(depth):
        proj = _inproj(x2d, mix_norm_g[i][None], w_in[i].astype(BF16), mu_shift[i][None], tm, s).reshape(b, s, -1)
        zeros = jnp.zeros((LANES - DECAY_LORA, w), F32)
        w2p = jnp.concatenate([decay_w2[i], zeros], axis=0)
        a2p = jnp.concatenate([zeros, iclr_a2[i]], axis=0)
        y_a = _rwkv(proj, decay_w0[i][None], w2p, iclr_a0[i][None], a2p, gate_g2[i],
                    k_k[i][None], k_a[i][None], r_k[i].reshape(1, w), ln_x_w[i][None], ln_x_b[i][None],
                    nrows=4 if b % 4 == 0 else 1, nchunks=2)
        y_b = _attention(proj, attn_out_g[i][None], shift_cols, attn_w)
        x2d = _ffn(x2d, y_a.reshape(b * s, w), y_b.reshape(b * s, attn_w), w_out[i].astype(BF16),
                   ffn_norm_g[i][None], w_gate[i].astype(BF16), w_up[i].astype(BF16),
                   w_down[i].astype(BF16), final_norm_g[None], tm, final=(i == depth - 1))
    return x2d.reshape(b, s, d)
```

```python
import functools

import jax
import jax.numpy as jnp
from jax import lax
from jax.experimental import pallas as pl
from jax.experimental.pallas import tpu as pltpu

HEAD_DIM = 64
LANES = 128
BF16_SUBLANES = 16
DECAY_LORA = 64
AAA_LORA = 64
GATE_LORA = 128
DILATED_PAIRS = ((128, 1), (512, 4), (2048, 16))
ATTN_BLOCK = 128
NORM_EPS = 1e-6
GN_EPS = 64e-5
CHUNK = 64
DECAY_SCALE = 0.6065306597126334
LOG2_E = 1.4426950408889634
VMEM_LIMIT_BYTES = 56 * 1024 * 1024
F32 = jnp.float32
BF16 = jnp.bfloat16


def _dot(a, b, dims=((1,), (0,))):
    return lax.dot_general(a.astype(BF16), b.astype(BF16), (dims, ((), ())),
                           preferred_element_type=F32)


_NT = ((1,), (1,))
_TN = ((0,), (0,))


def _rmsnorm(x, g):
    return x * lax.rsqrt(jnp.mean(x * x, axis=-1, keepdims=True) + NORM_EPS) * g


def _sigmoid(x):
    return 0.5 * jnp.tanh(0.5 * x) + 0.5


def _inproj_kernel(x_ref, g_ref, w_ref, mu_ref, o_ref, carry_ref, wb_ref, *, tiles_per_seq):
    tm = x_ref.shape[0]
    sc = mu_ref.shape[-1]

    @pl.when(pl.program_id(0) == 0)
    def _():
        for lo in range(0, w_ref.shape[1], LANES * 2):
            wb_ref[:, lo:lo + LANES * 2] = w_ref[:, lo:lo + LANES * 2].astype(BF16)

    @pl.when(pl.program_id(0) % tiles_per_seq == 0)
    def _():
        carry_ref[...] = jnp.zeros_like(carry_ref)

    h = _rmsnorm(x_ref[...], g_ref[...]).astype(BF16)
    p = jnp.dot(h, wb_ref[:, :sc], preferred_element_type=F32)
    first = lax.broadcasted_iota(jnp.int32, (tm, sc), 0) == 0
    prev = jnp.where(first, carry_ref[...], pltpu.roll(p, 1, axis=0))
    carry_ref[...] = p[tm - 1:tm, :]
    o_ref[:, :sc] = p + (prev - p) * mu_ref[...]
    o_ref[:, sc:] = jnp.dot(h, wb_ref[:, sc:], preferred_element_type=F32)


def _inproj(x2d, g, w, mu, tm, seq_len):
    t, d = x2d.shape
    n = w.shape[1]
    sc = mu.shape[-1]
    assert seq_len % tm == 0 and sc % LANES == 0 and n % (2 * LANES) == 0
    return pl.pallas_call(
        functools.partial(_inproj_kernel, tiles_per_seq=seq_len // tm),
        grid=(t // tm,),
        in_specs=[
            pl.BlockSpec((tm, d), lambda i: (i, 0)),
            pl.BlockSpec((1, d), lambda i: (0, 0)),
            pl.BlockSpec((d, n), lambda i: (0, 0), pipeline_mode=pl.Buffered(1)),
            pl.BlockSpec((1, sc), lambda i: (0, 0)),
        ],
        out_specs=pl.BlockSpec((tm, n), lambda i: (i, 0)),
        out_shape=jax.ShapeDtypeStruct((t, n), F32),
        scratch_shapes=[pltpu.VMEM((1, sc), F32), pltpu.VMEM((d, n), BF16)],
        compiler_params=pltpu.CompilerParams(
            dimension_semantics=("arbitrary",), vmem_limit_bytes=VMEM_LIMIT_BYTES),
        name="inproj",
    )(x2d, g, w, mu)


def _unit_lower_inverses(n_mats):
    c = n_mats[0].shape[0]
    ri = lax.broadcasted_iota(jnp.int32, (c, c), 0)
    ci = lax.broadcasted_iota(jnp.int32, (c, c), 1)
    same16 = (ri >> 4) == (ci >> 4)
    same32 = (ri >> 5) == (ci >> 5)
    eye = jnp.where(ri == ci, 1.0, 0.0).astype(F32)
    p = [jnp.where(same16, n, 0.0) for n in n_mats]
    t = [eye + n for n in p]
    for _ in range(3):
        p = [_dot(x, x) for x in p]
        t = [x + _dot(x, y) for x, y in zip(t, p)]
    for off in ([jnp.where(same32 & jnp.logical_not(same16), n, 0.0) for n in n_mats],
                [jnp.where(same32, 0.0, n) for n in n_mats]):
        q = [_dot(x, n) for x, n in zip(t, off)]
        t = [x + _dot(y, x) for x, y in zip(t, q)]
    return t


def _rwkv_kernel(p_ref, w0_ref, w2_ref, a0_ref, a2_ref, g2_ref, kk_ref, ka_ref, rk_ref,
                 lnw_ref, lnb_ref, *rest):
    n_cast = (len(rest) - 2) // 2
    o_ref, state_ref = rest[n_cast], rest[-1]
    for src, dst in zip(rest[:n_cast], rest[n_cast + 1:-1]):
        dst[...] = src[...].astype(BF16)

    c = CHUNK
    nrows = p_ref.shape[0]
    nchunks = p_ref.shape[1] // c
    w = w0_ref.shape[-1]
    n_pairs = w // LANES
    rc = nrows * c
    assert c == 64

    @pl.when(pl.program_id(1) == 0)
    def _():
        state_ref[...] = jnp.zeros_like(state_ref)

    first_head = lambda n: lax.broadcasted_iota(jnp.int32, (n, LANES), 1) < HEAD_DIM
    head0 = first_head(c)

    def head_sum(x):
        h0 = first_head(x.shape[0])
        s0 = jnp.sum(jnp.where(h0, x, 0.0), axis=-1, keepdims=True)
        s1 = jnp.sum(jnp.where(h0, 0.0, x), axis=-1, keepdims=True)
        return jnp.where(h0, s0, s1)

    def stack(x):
        return jnp.concatenate([jnp.where(head0, x, 0.0), jnp.where(head0, 0.0, x)], axis=0)

    ti = lax.broadcasted_iota(jnp.int32, (rc, rc), 0)
    si = lax.broadcasted_iota(jnp.int32, (rc, rc), 1)
    tri = jnp.where((si <= ti) & ((si >> 6) == (ti >> 6)), 1.0, 0.0).astype(BF16)
    t4 = lax.broadcasted_iota(jnp.int32, (4 * c, 4 * c), 0) & (c - 1)
    s4 = lax.broadcasted_iota(jnp.int32, (4 * c, 4 * c), 1) & (c - 1)
    is_r_row = lax.broadcasted_iota(jnp.int32, (4 * c, 4 * c), 0) >= 2 * c
    causal = (s4 < t4) | (is_r_row & (s4 == t4))

    units = [(b, j) for b in range(nrows) for j in range(n_pairs)]
    n_u = range(len(units))
    rows = lambda b: slice(b * c, (b + 1) * c)
    lanes = lambda j: slice(j * LANES, (j + 1) * LANES)

    def operand_stage(ch):
        cols = lambda lo, hi: jnp.concatenate(
            [p_ref[b, ch * c:(ch + 1) * c, lo:hi] for b in range(nrows)], axis=0)
        r = cols(0, w)
        k = cols(w, 2 * w)
        v = cols(2 * w, 3 * w)
        xwa = cols(3 * w, 3 * w + LANES)
        xg = cols(3 * w + LANES, 3 * w + 2 * LANES)

        z = w0_ref[...] + _dot(jnp.tanh(xwa), w2_ref[...])
        ld = -DECAY_SCALE * _sigmoid(z)
        a = _sigmoid(a0_ref[...] + _dot(xwa, a2_ref[...]))
        g = _dot(_sigmoid(xg), g2_ref[...])

        ld_hi = ld.astype(BF16)
        ld_lo = (ld - ld_hi.astype(F32)).astype(BF16)
        l_incl = _dot(tri, ld_hi) + _dot(tri, ld_lo)
        e_incl = jnp.exp(l_incl)
        e_excl = jnp.exp(l_incl - ld)
        e_inv = jnp.exp(-l_incl)

        kk_all = k * kk_ref[...]
        k2_all = k * (1.0 + (a - 1.0) * ka_ref[...])
        rk_all = r * k2_all * rk_ref[...]
        at_all, bt_all = [], []
        for j in range(n_pairs):
            sl = lanes(j)
            kk = kk_all[:, sl]
            kk = kk * lax.rsqrt(jnp.maximum(head_sum(kk * kk), 1e-24))
            at_all.append(-kk * e_excl[:, sl])
            bt_all.append(kk * a[:, sl] * e_inv[:, sl])
        rt_all = r * e_incl
        kt_all = k2_all * e_inv

        ar_s = [jnp.concatenate([stack(at_all[j][rows(b)]), stack(rt_all[rows(b), lanes(j)])],
                                axis=0).astype(BF16) for b, j in units]
        bk_s = [jnp.concatenate([stack(bt_all[j][rows(b)]), stack(kt_all[rows(b), lanes(j)])],
                                axis=0) for b, j in units]
        v_s = [stack(v[rows(b), lanes(j)]) for b, j in units]
        wc = [e_incl[b * c + c - 1:(b + 1) * c, lanes(j)] for b, j in units]
        bonus = [head_sum(rk_all[rows(b), lanes(j)]) * v[rows(b), lanes(j)] for b, j in units]
        gate = [g[rows(b), lanes(j)] for b, j in units]
        return ar_s, bk_s, v_s, wc, bonus, gate

    def state_free_stage(operands):
        ar_s, bk_s, v_s, wc, bonus, gate = operands
        m = [jnp.where(causal, _dot(ar_s[i], bk_s[i], _NT), 0.0) for i in n_u]
        t_inv = _unit_lower_inverses([m[i][:2 * c, :2 * c] for i in n_u])
        akv = [_dot(m[i][:2 * c, 2 * c:], v_s[i]) for i in n_u]
        m_r = [m[i][2 * c:, :] for i in n_u]
        return ar_s, bk_s, v_s, wc, t_inv, akv, m_r, bonus, gate

    def state_stage(ch, hs, ops):
        ar_s, bk_s, v_s, wc, t_inv, akv, m_r, bonus, gate = ops
        arh = [_dot(ar_s[i], hs[i], _NT) for i in n_u]
        u = [_dot(t_inv[i], arh[i][:2 * c] + akv[i]) for i in n_u]
        uv = [jnp.concatenate([u[i], v_s[i]], axis=0) for i in n_u]
        hs_new = [(hs[i] + _dot(uv[i], bk_s[i], _TN)) * wc[i] for i in n_u]
        y_s = [arh[i][2 * c:] + _dot(m_r[i], uv[i]) for i in n_u]
        for i, (b, j) in enumerate(units):
            sl = lanes(j)
            y = y_s[i][:c] + y_s[i][c:]
            mean = head_sum(y) * (1.0 / HEAD_DIM)
            yc = y - mean
            var = head_sum(yc * yc) * (1.0 / HEAD_DIM)
            yn = yc * lax.rsqrt(var + GN_EPS) * lnw_ref[:, sl] + lnb_ref[:, sl]
            o_ref[b, ch * c:(ch + 1) * c, sl] = (yn + bonus[i]) * gate[i]
        return hs_new

    operands = [operand_stage(ch) for ch in range(nchunks)]
    ops = [state_free_stage(x) for x in operands]
    hs = [state_ref[i] for i in n_u]
    for ch in range(nchunks):
        hs = state_stage(ch, hs, ops[ch])
    for i in n_u:
        state_ref[i] = hs[i]


def _rwkv(proj, w0, w2p, a0, a2p, g2, k_k, k_a, r_k, ln_w, ln_b, nrows, nchunks, to_bf16):
    b, s, _ = proj.shape
    w = w0.shape[-1]
    shift_cols = 3 * w + 2 * LANES
    c = CHUNK * nchunks
    grid = (b // nrows, s // c)
    steps = grid[0] * grid[1]
    row = lambda n: pl.BlockSpec((1, n), lambda bi, ci: (0, 0))
    mat = lambda m, n: pl.BlockSpec((m, n), lambda bi, ci: (0, 0))

    def cast_spec(x):
        parts = steps
        while x.shape[0] % (parts * BF16_SUBLANES):
            parts //= 2
        assert parts >= 1 and steps % parts == 0
        return pl.BlockSpec((x.shape[0] // parts, x.shape[1]),
                            lambda bi, ci: (jnp.minimum(bi * grid[1] + ci, parts - 1), 0))

    cast_specs = [cast_spec(x) for x in to_bf16]
    out = pl.pallas_call(
        _rwkv_kernel,
        grid=grid,
        in_specs=[
            pl.BlockSpec((nrows, c, shift_cols), lambda bi, ci: (bi, ci, 0)),
            row(w), mat(LANES, w), row(w), mat(LANES, w), mat(GATE_LORA, w),
            row(w), row(w), row(w), row(w), row(w),
        ] + cast_specs,
        out_specs=[pl.BlockSpec((nrows, c, w), lambda bi, ci: (bi, ci, 0))] + cast_specs,
        out_shape=[jax.ShapeDtypeStruct((b, s, w), F32)]
                  + [jax.ShapeDtypeStruct(x.shape, BF16) for x in to_bf16],
        scratch_shapes=[pltpu.VMEM((nrows * (w // LANES), LANES, LANES), F32)],
        compiler_params=pltpu.CompilerParams(
            dimension_semantics=("arbitrary", "arbitrary"), vmem_limit_bytes=VMEM_LIMIT_BYTES),
        name="rwkv7",
    )(proj, w0, w2p, a0, a2p, g2, k_k, k_a, r_k, ln_w, ln_b, *to_bf16)
    return out[0], out[1:]


def _attn_kernel(q_ref, k_ref, v_ref, g_ref, o_ref, ob_ref, lse_ref):
    s_len = q_ref.shape[0]
    blk = ATTN_BLOCK
    scale = HEAD_DIM ** -0.5
    lane = lax.broadcasted_iota(jnp.int32, (blk, LANES), 1)
    head0 = lane < HEAD_DIM

    def stack(x):
        return jnp.concatenate([jnp.where(head0, x, 0.0), jnp.where(head0, 0.0, x)], axis=0)

    def window_bias(span, has_prev):
        nk = 2 * blk if has_prev else blk
        qi = lax.broadcasted_iota(jnp.int32, (2 * blk, nk), 0) & (blk - 1)
        kj = lax.broadcasted_iota(jnp.int32, (2 * blk, nk), 1)
        rel = qi - kj + (blk if has_prev else 0)
        return jnp.where((rel >= 0) & (rel <= span), 0.0, -jnp.inf).astype(F32)

    def scores_stage(bi, dil, bias, blocks):
        ds = lambda s: pl.ds(s, blk, stride=dil) if dil > 1 else pl.ds(s, blk)
        rows = [ds(start) for start, _ in blocks]
        q_s = [stack(q_ref[r, :] * (scale * LOG2_E)).astype(BF16) for r in rows]
        kcat = [k_ref[r, :].astype(BF16) if ps is None else
                jnp.concatenate([k_ref[ds(ps), :], k_ref[r, :]], axis=0).astype(BF16)
                for r, (_, ps) in zip(rows, blocks)]
        vcat = [v_ref[r, :].astype(BF16) if ps is None else
                jnp.concatenate([v_ref[ds(ps), :], v_ref[r, :]], axis=0).astype(BF16)
                for r, (_, ps) in zip(rows, blocks)]
        sc = [_dot(q, kc, _NT) + bias[ps is not None]
              for q, kc, (_, ps) in zip(q_s, kcat, blocks)]
        return bi, rows, sc, vcat

    def softmax_stage(bi, rows, sc, vcat):
        mx = [jnp.max(x, axis=-1, keepdims=True) for x in sc]
        pe = [jnp.exp2(x - m) for x, m in zip(sc, mx)]
        den = [jnp.sum(x, axis=-1, keepdims=True) for x in pe]
        o_s = [_dot(x, vc) for x, vc in zip(pe, vcat)]
        res = []
        for o, m, d in zip(o_s, mx, den):
            d = jnp.where(head0, d[:blk], d[blk:])
            res.append((jnp.where(head0, o[:blk], o[blk:]) / d,
                        jnp.where(head0, m[:blk], m[blk:]) + jnp.log(d) * LOG2_E))
        return res

    add = lambda xs: functools.reduce(lambda x, y: x + y, xs)

    def store_stage(slot, rows, res):
        for r, (o, l) in zip(rows, res):
            ob_ref[slot, r, :] = o
            lse_ref[slot, r, :] = l

    def mix_stage(rows, res):
        for r, (o_here, l_here) in zip(rows, res):
            ls = [l_here] + [lse_ref[s, r, :] for s in range(ob_ref.shape[0])]
            os_ = [o_here] + [ob_ref[s, r, :] for s in range(ob_ref.shape[0])]
            mx = functools.reduce(jnp.maximum, ls)
            es = [jnp.exp2(l - mx) for l in ls]
            o = add([e * x for e, x in zip(es, os_)]) / add(es)
            o2 = o * o
            s0 = jnp.sum(jnp.where(head0, o2, 0.0), axis=-1, keepdims=True)
            s1 = jnp.sum(jnp.where(head0, 0.0, o2), axis=-1, keepdims=True)
            ms = jnp.where(head0, s0, s1) * (1.0 / HEAD_DIM)
            o_ref[r, :] = o * lax.rsqrt(ms + NORM_EPS) * g_ref[...]

    group = 2
    groups = []
    order = sorted(range(len(DILATED_PAIRS)), key=lambda b: -DILATED_PAIRS[b][1])
    assert DILATED_PAIRS[order[-1]][1] == 1
    for bi in order:
        window, dil = DILATED_PAIRS[bi]
        sub_len = s_len // dil
        span = window // dil
        assert sub_len % blk == 0
        nb = sub_len // blk
        blocks = [(n * blk * dil + r, (n - 1) * blk * dil + r if n else None)
                  for r in range(dil) for n in range(nb)]
        assert len(blocks) % group == 0
        groups += [(bi, dil, span, nb, blocks[i:i + group]) for i in range(0, len(blocks), group)]

    def finish(bi, rows, sc, vcat):
        res = softmax_stage(bi, rows, sc, vcat)
        if bi == order[-1]:
            mix_stage(rows, res)
        else:
            store_stage(order.index(bi), rows, res)

    biases = {}
    pending = None
    for bi, dil, span, nb, blocks in groups:
        if (span, nb) not in biases:
            biases[span, nb] = {False: window_bias(span, False),
                                True: window_bias(span, True) if nb > 1 else None}
        nxt = scores_stage(bi, dil, biases[span, nb], blocks)
        if pending is not None:
            finish(*pending)
        pending = nxt
    finish(*pending)


def _attention(proj, out_g, q_col0, width):
    b, s, _ = proj.shape
    n_pairs = width // LANES
    qb = q_col0 // LANES
    spec = lambda off: pl.BlockSpec((None, s, LANES), lambda bi, pi: (bi, 0, off + pi))
    return pl.pallas_call(
        _attn_kernel,
        grid=(b, n_pairs),
        in_specs=[spec(qb), spec(qb + n_pairs), spec(qb + 2 * n_pairs),
                  pl.BlockSpec((1, LANES), lambda bi, pi: (0, pi))],
        out_specs=pl.BlockSpec((None, s, LANES), lambda bi, pi: (bi, 0, pi)),
        out_shape=jax.ShapeDtypeStruct((b, s, width), F32),
        scratch_shapes=[
            pltpu.VMEM((len(DILATED_PAIRS) - 1, s, LANES), F32),
            pltpu.VMEM((len(DILATED_PAIRS) - 1, s, LANES), F32),
        ],
        compiler_params=pltpu.CompilerParams(
            dimension_semantics=("arbitrary", "arbitrary"), vmem_limit_bytes=VMEM_LIMIT_BYTES),
        name="dilated_attn",
    )(proj, proj, proj, out_g)


def _ffn_kernel(x_ref, ya_ref, yb_ref, wo_ref, gn_ref, wg_ref, wu_ref, wd_ref, gf_ref, o_ref, *, final):
    wa = ya_ref.shape[-1]
    x1 = (x_ref[...]
          + jnp.dot(ya_ref[...].astype(BF16), wo_ref[0:wa, :], preferred_element_type=F32)
          + jnp.dot(yb_ref[...].astype(BF16), wo_ref[wa:, :], preferred_element_type=F32))
    h = _rmsnorm(x1, gn_ref[...]).astype(BF16)
    gate = jnp.dot(h, wg_ref[...], preferred_element_type=F32)
    up = jnp.dot(h, wu_ref[...], preferred_element_type=F32)
    act = (gate * _sigmoid(gate)) * up
    x2 = x1 + jnp.dot(act.astype(BF16), wd_ref[...], preferred_element_type=F32)
    o_ref[...] = _rmsnorm(x2, gf_ref[...]) if final else x2


def _ffn(x2d, ya, yb, wo, gn, wg, wu, wd, gf, tm, final):
    t, d = x2d.shape
    wa, wb = ya.shape[-1], yb.shape[-1]
    dff = wg.shape[-1]
    const = lambda m, n: pl.BlockSpec((m, n), lambda i: (0, 0), pipeline_mode=pl.Buffered(1))
    return pl.pallas_call(
        functools.partial(_ffn_kernel, final=final),
        grid=(t // tm,),
        in_specs=[
            pl.BlockSpec((tm, d), lambda i: (i, 0)),
            pl.BlockSpec((tm, wa), lambda i: (i, 0)),
            pl.BlockSpec((tm, wb), lambda i: (i, 0)),
            const(wa + wb, d), const(1, d), const(d, dff), const(d, dff), const(dff, d), const(1, d),
        ],
        out_specs=pl.BlockSpec((tm, d), lambda i: (i, 0)),
        out_shape=jax.ShapeDtypeStruct((t, d), F32),
        compiler_params=pltpu.CompilerParams(
            dimension_semantics=("arbitrary",), vmem_limit_bytes=VMEM_LIMIT_BYTES),
        name="outproj_ffn",
    )(x2d, ya, yb, wo, gn, wg, wu, wd, gf)


def kernel(x, mix_norm_g, w_in, mu_shift, decay_w0, decay_w2, iclr_a0, iclr_a2, gate_g2, k_k, k_a, r_k,
           ln_x_w, ln_x_b, attn_out_g, w_out, ffn_norm_g, w_gate, w_up, w_down, final_norm_g):
    b, s, d = x.shape
    depth = w_in.shape[0]
    w = decay_w0.shape[-1]
    shift_cols = mu_shift.shape[-1]
    attn_w = attn_out_g.shape[-1]
    assert shift_cols == 3 * w + DECAY_LORA + AAA_LORA + GATE_LORA
    assert DECAY_LORA + AAA_LORA == LANES and shift_cols % LANES == 0
    tm = 512
    x2d = x.reshape(b * s, d)
    for i in range(depth):
        proj = _inproj(x2d, mix_norm_g[i][None], w_in[i], mu_shift[i][None], tm, s).reshape(b, s, -1)
        zeros = jnp.zeros((LANES - DECAY_LORA, w), F32)
        w2p = jnp.concatenate([decay_w2[i], zeros], axis=0)
        a2p = jnp.concatenate([zeros, iclr_a2[i]], axis=0)
        y_a, (wo, wg, wu, wd) = _rwkv(
            proj, decay_w0[i][None], w2p, iclr_a0[i][None], a2p, gate_g2[i],
            k_k[i][None], k_a[i][None], r_k[i].reshape(1, w), ln_x_w[i][None], ln_x_b[i][None],
            nrows=4 if b % 4 == 0 else 1, nchunks=2, to_bf16=(w_out[i], w_gate[i], w_up[i], w_down[i]))
        y_b = _attention(proj, attn_out_g[i][None], shift_cols, attn_w)
        x2d = _ffn(x2d, y_a.reshape(b * s, w), y_b.reshape(b * s, attn_w), wo, ffn_norm_g[i][None],
                   wg, wu, wd, final_norm_g[None], tm, final=(i == depth - 1))
    return x2d.reshape(b, s, d)
```

```python
import functools

import jax
import jax.numpy as jnp
from jax import lax
from jax.experimental import pallas as pl
from jax.experimental.pallas import tpu as pltpu

HEAD_DIM = 64
LANES = 128
BF16_SUBLANES = 16
DECAY_LORA = 64
AAA_LORA = 64
GATE_LORA = 128
DILATED_PAIRS = ((128, 1), (512, 4), (2048, 16))
ATTN_BLOCK = 128
NORM_EPS = 1e-6
GN_EPS = 64e-5
CHUNK = 64
INVERSE_BASE_BLOCK = 16
KK_NORM_FLOOR = 1e-12
TOKEN_TILE = 512
RWKV_ROWS_PER_STEP = 4
RWKV_CHUNKS_PER_STEP = 2
ATTN_GROUP = 4
DECAY_SCALE = 0.6065306597126334
LOG2_E = 1.4426950408889634
VMEM_LIMIT_BYTES = 56 * 1024 * 1024
F32 = jnp.float32
BF16 = jnp.bfloat16


def _dot(a, b, dims=((1,), (0,))):
    return lax.dot_general(a.astype(BF16), b.astype(BF16), (dims, ((), ())),
                           preferred_element_type=F32)


_NT = ((1,), (1,))
_TN = ((0,), (0,))


def _rmsnorm(x, g):
    return x * lax.rsqrt(jnp.mean(x * x, axis=-1, keepdims=True) + NORM_EPS) * g


def _sigmoid(x):
    return 0.5 * jnp.tanh(0.5 * x) + 0.5


def _inproj_kernel(x_ref, g_ref, w_ref, mu_ref, o_ref, carry_ref, wb_ref, *, tiles_per_seq):
    tm = x_ref.shape[0]
    sc = mu_ref.shape[-1]

    @pl.when(pl.program_id(0) == 0)
    def _():
        for lo in range(0, w_ref.shape[1], LANES * 2):
            wb_ref[:, lo:lo + LANES * 2] = w_ref[:, lo:lo + LANES * 2].astype(BF16)

    @pl.when(pl.program_id(0) % tiles_per_seq == 0)
    def _():
        carry_ref[...] = jnp.zeros_like(carry_ref)

    h = _rmsnorm(x_ref[...], g_ref[...]).astype(BF16)
    p = jnp.dot(h, wb_ref[:, :sc], preferred_element_type=F32)
    first = lax.broadcasted_iota(jnp.int32, (tm, sc), 0) == 0
    prev = jnp.where(first, carry_ref[...], pltpu.roll(p, 1, axis=0))
    carry_ref[...] = p[tm - 1:tm, :]
    o_ref[:, :sc] = p + (prev - p) * mu_ref[...]
    o_ref[:, sc:] = jnp.dot(h, wb_ref[:, sc:], preferred_element_type=F32)


def _inproj(x2d, g, w, mu, tm, seq_len):
    t, d = x2d.shape
    n = w.shape[1]
    sc = mu.shape[-1]
    assert seq_len % tm == 0 and sc % LANES == 0 and n % (2 * LANES) == 0
    return pl.pallas_call(
        functools.partial(_inproj_kernel, tiles_per_seq=seq_len // tm),
        grid=(t // tm,),
        in_specs=[
            pl.BlockSpec((tm, d), lambda i: (i, 0)),
            pl.BlockSpec((1, d), lambda i: (0, 0)),
            pl.BlockSpec((d, n), lambda i: (0, 0), pipeline_mode=pl.Buffered(1)),
            pl.BlockSpec((1, sc), lambda i: (0, 0)),
        ],
        out_specs=pl.BlockSpec((tm, n), lambda i: (i, 0)),
        out_shape=jax.ShapeDtypeStruct((t, n), F32),
        scratch_shapes=[pltpu.VMEM((1, sc), F32), pltpu.VMEM((d, n), BF16)],
        compiler_params=pltpu.CompilerParams(
            dimension_semantics=("arbitrary",), vmem_limit_bytes=VMEM_LIMIT_BYTES),
        name="inproj",
    )(x2d, g, w, mu)


def _unit_lower_inverses(n_mats):
    c = n_mats[0].shape[0]
    ri = lax.broadcasted_iota(jnp.int32, (c, c), 0)
    ci = lax.broadcasted_iota(jnp.int32, (c, c), 1)
    same = lambda size: (ri >> (size.bit_length() - 1)) == (ci >> (size.bit_length() - 1))
    eye = jnp.where(ri == ci, 1.0, 0.0).astype(F32)
    base = INVERSE_BASE_BLOCK
    p = [jnp.where(same(base), n, 0.0) for n in n_mats]
    t = [eye + n for n in p]
    for _ in range(base.bit_length() - 2):
        p = [_dot(x, x) for x in p]
        t = [x + _dot(x, y) for x, y in zip(t, p)]
    size = base
    while size < CHUNK:
        off = [jnp.where(same(2 * size) & jnp.logical_not(same(size)), n, 0.0) for n in n_mats]
        q = [_dot(x, n) for x, n in zip(t, off)]
        t = [x + _dot(y, x) for x, y in zip(t, q)]
        size *= 2
    return t


def _rwkv_kernel(p_ref, w0_ref, w2_ref, a0_ref, a2_ref, g2_ref, kk_ref, ka_ref, rk_ref,
                 lnw_ref, lnb_ref, *rest):
    n_cast = (len(rest) - 2) // 2
    o_ref, state_ref = rest[n_cast], rest[-1]
    for src, dst in zip(rest[:n_cast], rest[n_cast + 1:-1]):
        dst[...] = src[...].astype(BF16)

    c = CHUNK
    nrows = p_ref.shape[0]
    nchunks = p_ref.shape[1] // c
    w = w0_ref.shape[-1]
    n_pairs = w // LANES
    rc = nrows * c
    assert c & (c - 1) == 0 and c >= INVERSE_BASE_BLOCK
    log2_c = c.bit_length() - 1

    @pl.when(pl.program_id(1) == 0)
    def _():
        state_ref[...] = jnp.zeros_like(state_ref)

    first_head = lambda n: lax.broadcasted_iota(jnp.int32, (n, LANES), 1) < HEAD_DIM
    head0 = first_head(c)

    def head_sum(x):
        h0 = first_head(x.shape[0])
        s0 = jnp.sum(jnp.where(h0, x, 0.0), axis=-1, keepdims=True)
        s1 = jnp.sum(jnp.where(h0, 0.0, x), axis=-1, keepdims=True)
        return jnp.where(h0, s0, s1)

    def stack(x):
        return jnp.concatenate([jnp.where(head0, x, 0.0), jnp.where(head0, 0.0, x)], axis=0)

    ti = lax.broadcasted_iota(jnp.int32, (rc, rc), 0)
    si = lax.broadcasted_iota(jnp.int32, (rc, rc), 1)
    tri = jnp.where((si <= ti) & ((si >> log2_c) == (ti >> log2_c)), 1.0, 0.0).astype(BF16)
    t4 = lax.broadcasted_iota(jnp.int32, (4 * c, 4 * c), 0) & (c - 1)
    s4 = lax.broadcasted_iota(jnp.int32, (4 * c, 4 * c), 1) & (c - 1)
    is_r_row = lax.broadcasted_iota(jnp.int32, (4 * c, 4 * c), 0) >= 2 * c
    causal = (s4 < t4) | (is_r_row & (s4 == t4))

    units = [(b, j) for b in range(nrows) for j in range(n_pairs)]
    n_u = range(len(units))
    rows = lambda b: slice(b * c, (b + 1) * c)
    lanes = lambda j: slice(j * LANES, (j + 1) * LANES)

    def operand_stage(ch):
        cols = lambda lo, hi: jnp.concatenate(
            [p_ref[b, ch * c:(ch + 1) * c, lo:hi] for b in range(nrows)], axis=0)
        r = cols(0, w)
        k = cols(w, 2 * w)
        v = cols(2 * w, 3 * w)
        xwa = cols(3 * w, 3 * w + LANES)
        xg = cols(3 * w + LANES, 3 * w + 2 * LANES)

        z = w0_ref[...] + _dot(jnp.tanh(xwa), w2_ref[...])
        ld = -DECAY_SCALE * _sigmoid(z)
        a = _sigmoid(a0_ref[...] + _dot(xwa, a2_ref[...]))
        g = _dot(_sigmoid(xg), g2_ref[...])

        ld_hi = ld.astype(BF16)
        ld_lo = (ld - ld_hi.astype(F32)).astype(BF16)
        l_incl = _dot(tri, ld_hi) + _dot(tri, ld_lo)
        e_incl = jnp.exp(l_incl)
        e_excl = jnp.exp(l_incl - ld)
        e_inv = jnp.exp(-l_incl)

        kk_all = k * kk_ref[...]
        k2_all = k * (1.0 + (a - 1.0) * ka_ref[...])
        rk_all = r * k2_all * rk_ref[...]
        at_all, bt_all = [], []
        for j in range(n_pairs):
            sl = lanes(j)
            kk = kk_all[:, sl]
            kk = kk * lax.rsqrt(jnp.maximum(head_sum(kk * kk), KK_NORM_FLOOR ** 2))
            at_all.append(-kk * e_excl[:, sl])
            bt_all.append(kk * a[:, sl] * e_inv[:, sl])
        rt_all = r * e_incl
        kt_all = k2_all * e_inv

        ar_s = [jnp.concatenate([stack(at_all[j][rows(b)]), stack(rt_all[rows(b), lanes(j)])],
                                axis=0).astype(BF16) for b, j in units]
        bk_s = [jnp.concatenate([stack(bt_all[j][rows(b)]), stack(kt_all[rows(b), lanes(j)])],
                                axis=0) for b, j in units]
        v_s = [stack(v[rows(b), lanes(j)]) for b, j in units]
        wc = [e_incl[b * c + c - 1:(b + 1) * c, lanes(j)] for b, j in units]
        bonus = [head_sum(rk_all[rows(b), lanes(j)]) * v[rows(b), lanes(j)] for b, j in units]
        gate = [g[rows(b), lanes(j)] for b, j in units]
        return ar_s, bk_s, v_s, wc, bonus, gate

    def state_free_stage(operands):
        ar_s, bk_s, v_s, wc, bonus, gate = operands
        m = [jnp.where(causal, _dot(ar_s[i], bk_s[i], _NT), 0.0) for i in n_u]
        t_inv = _unit_lower_inverses([m[i][:2 * c, :2 * c] for i in n_u])
        akv = [_dot(m[i][:2 * c, 2 * c:], v_s[i]) for i in n_u]
        m_r = [m[i][2 * c:, :] for i in n_u]
        return ar_s, bk_s, v_s, wc, t_inv, akv, m_r, bonus, gate

    def state_stage(ch, hs, ops):
        ar_s, bk_s, v_s, wc, t_inv, akv, m_r, bonus, gate = ops
        arh = [_dot(ar_s[i], hs[i], _NT) for i in n_u]
        u = [_dot(t_inv[i], arh[i][:2 * c] + akv[i]) for i in n_u]
        uv = [jnp.concatenate([u[i], v_s[i]], axis=0) for i in n_u]
        hs_new = [(hs[i] + _dot(uv[i], bk_s[i], _TN)) * wc[i] for i in n_u]
        y_s = [arh[i][2 * c:] + _dot(m_r[i], uv[i]) for i in n_u]
        for i, (b, j) in enumerate(units):
            sl = lanes(j)
            y = y_s[i][:c] + y_s[i][c:]
            mean = head_sum(y) * (1.0 / HEAD_DIM)
            yc = y - mean
            var = head_sum(yc * yc) * (1.0 / HEAD_DIM)
            yn = yc * lax.rsqrt(var + GN_EPS) * lnw_ref[:, sl] + lnb_ref[:, sl]
            o_ref[b, ch * c:(ch + 1) * c, sl] = (yn + bonus[i]) * gate[i]
        return hs_new

    operands = [operand_stage(ch) for ch in range(nchunks)]
    ops = [state_free_stage(x) for x in operands]
    hs = [state_ref[i] for i in n_u]
    for ch in range(nchunks):
        hs = state_stage(ch, hs, ops[ch])
    for i in n_u:
        state_ref[i] = hs[i]


def _rwkv(proj, w0, w2p, a0, a2p, g2, k_k, k_a, r_k, ln_w, ln_b, nrows, nchunks, to_bf16):
    b, s, _ = proj.shape
    w = w0.shape[-1]
    shift_cols = 3 * w + 2 * LANES
    c = CHUNK * nchunks
    grid = (b // nrows, s // c)
    steps = grid[0] * grid[1]
    row = lambda n: pl.BlockSpec((1, n), lambda bi, ci: (0, 0))
    mat = lambda m, n: pl.BlockSpec((m, n), lambda bi, ci: (0, 0))

    def cast_spec(x):
        parts = steps
        while x.shape[0] % (parts * BF16_SUBLANES):
            parts //= 2
        assert parts >= 1 and steps % parts == 0
        return pl.BlockSpec((x.shape[0] // parts, x.shape[1]),
                            lambda bi, ci: (jnp.minimum(bi * grid[1] + ci, parts - 1), 0))

    cast_specs = [cast_spec(x) for x in to_bf16]
    out = pl.pallas_call(
        _rwkv_kernel,
        grid=grid,
        in_specs=[
            pl.BlockSpec((nrows, c, shift_cols), lambda bi, ci: (bi, ci, 0)),
            row(w), mat(LANES, w), row(w), mat(LANES, w), mat(GATE_LORA, w),
            row(w), row(w), row(w), row(w), row(w),
        ] + cast_specs,
        out_specs=[pl.BlockSpec((nrows, c, w), lambda bi, ci: (bi, ci, 0))] + cast_specs,
        out_shape=[jax.ShapeDtypeStruct((b, s, w), F32)]
                  + [jax.ShapeDtypeStruct(x.shape, BF16) for x in to_bf16],
        scratch_shapes=[pltpu.VMEM((nrows * (w // LANES), LANES, LANES), F32)],
        compiler_params=pltpu.CompilerParams(
            dimension_semantics=("arbitrary", "arbitrary"), vmem_limit_bytes=VMEM_LIMIT_BYTES),
        name="rwkv7",
    )(proj, w0, w2p, a0, a2p, g2, k_k, k_a, r_k, ln_w, ln_b, *to_bf16)
    return out[0], out[1:]


def _attn_kernel(q_ref, k_ref, v_ref, g_ref, o_ref, ob_ref, mx_ref, den_ref):
    s_len = q_ref.shape[0]
    blk = ATTN_BLOCK
    scale = HEAD_DIM ** -0.5
    lane = lax.broadcasted_iota(jnp.int32, (blk, LANES), 1)
    head0 = lane < HEAD_DIM

    def stack(x):
        return jnp.concatenate([jnp.where(head0, x, 0.0), jnp.where(head0, 0.0, x)], axis=0)

    def window_bias(span, has_prev):
        nk = 2 * blk if has_prev else blk
        qi = lax.broadcasted_iota(jnp.int32, (2 * blk, nk), 0) & (blk - 1)
        kj = lax.broadcasted_iota(jnp.int32, (2 * blk, nk), 1)
        rel = qi - kj + (blk if has_prev else 0)
        return jnp.where((rel >= 0) & (rel <= span), 0.0, -jnp.inf).astype(F32)

    def scores_stage(bi, dil, bias, blocks):
        ds = lambda s: pl.ds(s, blk, stride=dil) if dil > 1 else pl.ds(s, blk)
        rows = [ds(start) for start, _ in blocks]
        q_s = [stack(q_ref[r, :] * (scale * LOG2_E)).astype(BF16) for r in rows]
        kcat = [k_ref[r, :].astype(BF16) if ps is None else
                jnp.concatenate([k_ref[ds(ps), :], k_ref[r, :]], axis=0).astype(BF16)
                for r, (_, ps) in zip(rows, blocks)]
        vcat = [v_ref[r, :].astype(BF16) if ps is None else
                jnp.concatenate([v_ref[ds(ps), :], v_ref[r, :]], axis=0).astype(BF16)
                for r, (_, ps) in zip(rows, blocks)]
        sc = [_dot(q, kc, _NT) + bias[ps is not None]
              for q, kc, (_, ps) in zip(q_s, kcat, blocks)]
        return bi, rows, sc, vcat

    def softmax_stage(bi, rows, sc, vcat):
        mx = [jnp.max(x, axis=-1, keepdims=True) for x in sc]
        pe = [jnp.exp2(x - m).astype(BF16) for x, m in zip(sc, mx)]
        return bi, rows, pe, mx, vcat

    def pv_stage(pe, mx, vcat):
        ones = jnp.ones((2 * blk, LANES), BF16)
        ov = [_dot(x, jnp.concatenate([vc, ones[:vc.shape[0]]], axis=1)) for x, vc in zip(pe, vcat)]
        pair = lambda x: jnp.where(head0, x[:blk], x[blk:])
        return [(pair(x[:, :LANES]), pair(m), pair(x[:, LANES:])) for x, m in zip(ov, mx)]

    add = lambda xs: functools.reduce(lambda x, y: x + y, xs)

    def store_stage(slot, rows, res):
        for r, (o, m, d) in zip(rows, res):
            ob_ref[slot, r, :] = o
            mx_ref[slot, r, :] = m
            den_ref[slot, r, :] = d

    def mix_stage(rows, res):
        slots = range(ob_ref.shape[0])
        for r, (o_here, m_here, d_here) in zip(rows, res):
            ms_ = [m_here] + [mx_ref[s, r, :] for s in slots]
            os_ = [o_here] + [ob_ref[s, r, :] for s in slots]
            ds_ = [d_here] + [den_ref[s, r, :] for s in slots]
            top = functools.reduce(jnp.maximum, ms_)
            es = [jnp.exp2(m - top) for m in ms_]
            o = add([e * x for e, x in zip(es, os_)]) / add([e * d for e, d in zip(es, ds_)])
            o2 = o * o
            s0 = jnp.sum(jnp.where(head0, o2, 0.0), axis=-1, keepdims=True)
            s1 = jnp.sum(jnp.where(head0, 0.0, o2), axis=-1, keepdims=True)
            ms = jnp.where(head0, s0, s1) * (1.0 / HEAD_DIM)
            o_ref[r, :] = o * lax.rsqrt(ms + NORM_EPS) * g_ref[...]

    group = ATTN_GROUP
    groups = []
    order = sorted(range(len(DILATED_PAIRS)), key=lambda b: -DILATED_PAIRS[b][1])
    assert DILATED_PAIRS[order[-1]][1] == 1
    for bi in order:
        window, dil = DILATED_PAIRS[bi]
        sub_len = s_len // dil
        span = window // dil
        assert sub_len % blk == 0
        nb = sub_len // blk
        blocks = [(n * blk * dil + r, (n - 1) * blk * dil + r if n else None)
                  for r in range(dil) for n in range(nb)]
        assert len(blocks) % group == 0
        groups += [(bi, dil, span, nb, blocks[i:i + group]) for i in range(0, len(blocks), group)]

    def finish(bi, rows, pe, mx, vcat):
        res = pv_stage(pe, mx, vcat)
        if bi == order[-1]:
            mix_stage(rows, res)
        else:
            store_stage(order.index(bi), rows, res)

    biases = {}
    scored, soft = None, None
    for bi, dil, span, nb, blocks in groups + [(None,) * 5] * 2:
        nxt = None
        if blocks is not None:
            if (span, nb) not in biases:
                biases[span, nb] = {False: window_bias(span, False),
                                    True: window_bias(span, True) if nb > 1 else None}
            nxt = scores_stage(bi, dil, biases[span, nb], blocks)
        nxt_soft = softmax_stage(*scored) if scored is not None else None
        if soft is not None:
            finish(*soft)
        scored, soft = nxt, nxt_soft


def _attention(proj, out_g, q_col0, width):
    b, s, _ = proj.shape
    n_pairs = width // LANES
    qb = q_col0 // LANES
    spec = lambda off: pl.BlockSpec((None, s, LANES), lambda bi, pi: (bi, 0, off + pi))
    return pl.pallas_call(
        _attn_kernel,
        grid=(b, n_pairs),
        in_specs=[spec(qb), spec(qb + n_pairs), spec(qb + 2 * n_pairs),
                  pl.BlockSpec((1, LANES), lambda bi, pi: (0, pi))],
        out_specs=pl.BlockSpec((None, s, LANES), lambda bi, pi: (bi, 0, pi)),
        out_shape=jax.ShapeDtypeStruct((b, s, width), F32),
        scratch_shapes=[
            pltpu.VMEM((len(DILATED_PAIRS) - 1, s, LANES), F32) for _ in range(3)],
        compiler_params=pltpu.CompilerParams(
            dimension_semantics=("arbitrary", "arbitrary"), vmem_limit_bytes=VMEM_LIMIT_BYTES),
        name="dilated_attn",
    )(proj, proj, proj, out_g)


def _ffn_kernel(x_ref, ya_ref, yb_ref, wo_ref, gn_ref, wg_ref, wu_ref, wd_ref, gf_ref, o_ref, *, final):
    wa = ya_ref.shape[-1]
    x1 = (x_ref[...]
          + jnp.dot(ya_ref[...].astype(BF16), wo_ref[0:wa, :], preferred_element_type=F32)
          + jnp.dot(yb_ref[...].astype(BF16), wo_ref[wa:, :], preferred_element_type=F32))
    h = _rmsnorm(x1, gn_ref[...]).astype(BF16)
    gate = jnp.dot(h, wg_ref[...], preferred_element_type=F32)
    up = jnp.dot(h, wu_ref[...], preferred_element_type=F32)
    act = (gate * _sigmoid(gate)) * up
    x2 = x1 + jnp.dot(act.astype(BF16), wd_ref[...], preferred_element_type=F32)
    o_ref[...] = _rmsnorm(x2, gf_ref[...]) if final else x2


def _ffn(x2d, ya, yb, wo, gn, wg, wu, wd, gf, tm, final):
    t, d = x2d.shape
    wa, wb = ya.shape[-1], yb.shape[-1]
    dff = wg.shape[-1]
    const = lambda m, n: pl.BlockSpec((m, n), lambda i: (0, 0), pipeline_mode=pl.Buffered(1))
    return pl.pallas_call(
        functools.partial(_ffn_kernel, final=final),
        grid=(t // tm,),
        in_specs=[
            pl.BlockSpec((tm, d), lambda i: (i, 0)),
            pl.BlockSpec((tm, wa), lambda i: (i, 0)),
            pl.BlockSpec((tm, wb), lambda i: (i, 0)),
            const(wa + wb, d), const(1, d), const(d, dff), const(d, dff), const(dff, d), const(1, d),
        ],
        out_specs=pl.BlockSpec((tm, d), lambda i: (i, 0)),
        out_shape=jax.ShapeDtypeStruct((t, d), F32),
        compiler_params=pltpu.CompilerParams(
            dimension_semantics=("arbitrary",), vmem_limit_bytes=VMEM_LIMIT_BYTES),
        name="outproj_ffn",
    )(x2d, ya, yb, wo, gn, wg, wu, wd, gf)


def kernel(x, mix_norm_g, w_in, mu_shift, decay_w0, decay_w2, iclr_a0, iclr_a2, gate_g2, k_k, k_a, r_k,
           ln_x_w, ln_x_b, attn_out_g, w_out, ffn_norm_g, w_gate, w_up, w_down, final_norm_g):
    b, s, d = x.shape
    depth = w_in.shape[0]
    w = decay_w0.shape[-1]
    shift_cols = mu_shift.shape[-1]
    attn_w = attn_out_g.shape[-1]
    assert shift_cols == 3 * w + DECAY_LORA + AAA_LORA + GATE_LORA
    assert DECAY_LORA + AAA_LORA == LANES and shift_cols % LANES == 0
    tm = TOKEN_TILE
    rows_per_step = RWKV_ROWS_PER_STEP if b % RWKV_ROWS_PER_STEP == 0 else 1
    x2d = x.reshape(b * s, d)
    for i in range(depth):
        proj = _inproj(x2d, mix_norm_g[i][None], w_in[i], mu_shift[i][None], tm, s).reshape(b, s, -1)
        zeros = jnp.zeros((LANES - DECAY_LORA, w), F32)
        w2p = jnp.concatenate([decay_w2[i], zeros], axis=0)
        a2p = jnp.concatenate([zeros, iclr_a2[i]], axis=0)
        y_a, (wo, wg, wu, wd) = _rwkv(
            proj, decay_w0[i][None], w2p, iclr_a0[i][None], a2p, gate_g2[i],
            k_k[i][None], k_a[i][None], r_k[i].reshape(1, w), ln_x_w[i][None], ln_x_b[i][None],
            nrows=rows_per_step, nchunks=RWKV_CHUNKS_PER_STEP,
            to_bf16=(w_out[i], w_gate[i], w_up[i], w_down[i]))
        y_b = _attention(proj, attn_out_g[i][None], shift_cols, attn_w)
        x2d = _ffn(x2d, y_a.reshape(b * s, w), y_b.reshape(b * s, attn_w), wo, ffn_norm_g[i][None],
                   wg, wu, wd, final_norm_g[None], tm, final=(i == depth - 1))
    return x2d.reshape(b, s, d)
```

```python
import functools

import jax
import jax.numpy as jnp
from jax import lax
from jax.experimental import pallas as pl
from jax.experimental.pallas import tpu as pltpu

HEAD_DIM = 64
LANES = 128
BF16_SUBLANES = 16
DECAY_LORA = 64
AAA_LORA = 64
GATE_LORA = 128
DILATED_PAIRS = ((128, 1), (512, 4), (2048, 16))
ATTN_BLOCK = 128
NORM_EPS = 1e-6
GN_EPS = 64e-5
CHUNK = 64
INVERSE_BASE_BLOCK = 16
KK_NORM_FLOOR = 1e-12
TOKEN_TILE = 512
RWKV_ROWS_PER_STEP = 4
RWKV_CHUNKS_PER_STEP = 2
ATTN_GROUP = 4
DECAY_SCALE = 0.6065306597126334
LOG2_E = 1.4426950408889634
VMEM_LIMIT_BYTES = 56 * 1024 * 1024
F32 = jnp.float32
BF16 = jnp.bfloat16


def _dot(a, b, dims=((1,), (0,))):
    return lax.dot_general(a.astype(BF16), b.astype(BF16), (dims, ((), ())),
                           preferred_element_type=F32)


_NT = ((1,), (1,))
_TN = ((0,), (0,))


def _rmsnorm(x, g):
    return x * lax.rsqrt(jnp.mean(x * x, axis=-1, keepdims=True) + NORM_EPS) * g


def _sigmoid(x):
    return 0.5 * jnp.tanh(0.5 * x) + 0.5


def _inproj_kernel(x_ref, g_ref, w_ref, mu_ref, o_ref, carry_ref, wb_ref, *, tiles_per_seq):
    tm = x_ref.shape[0]
    sc = mu_ref.shape[-1]

    @pl.when(pl.program_id(0) == 0)
    def _():
        for lo in range(0, w_ref.shape[1], LANES * 2):
            wb_ref[:, lo:lo + LANES * 2] = w_ref[:, lo:lo + LANES * 2].astype(BF16)

    @pl.when(pl.program_id(0) % tiles_per_seq == 0)
    def _():
        carry_ref[...] = jnp.zeros_like(carry_ref)

    h = _rmsnorm(x_ref[...], g_ref[...]).astype(BF16)
    p = jnp.dot(h, wb_ref[:, :sc], preferred_element_type=F32)
    first = lax.broadcasted_iota(jnp.int32, (tm, sc), 0) == 0
    prev = jnp.where(first, carry_ref[...], pltpu.roll(p, 1, axis=0))
    carry_ref[...] = p[tm - 1:tm, :]
    o_ref[:, :sc] = p + (prev - p) * mu_ref[...]
    o_ref[:, sc:] = jnp.dot(h, wb_ref[:, sc:], preferred_element_type=F32)


def _inproj(x2d, g, w, mu, tm, seq_len):
    t, d = x2d.shape
    n = w.shape[1]
    sc = mu.shape[-1]
    assert seq_len % tm == 0 and sc % LANES == 0 and n % (2 * LANES) == 0
    return pl.pallas_call(
        functools.partial(_inproj_kernel, tiles_per_seq=seq_len // tm),
        grid=(t // tm,),
        in_specs=[
            pl.BlockSpec((tm, d), lambda i: (i, 0)),
            pl.BlockSpec((1, d), lambda i: (0, 0)),
            pl.BlockSpec((d, n), lambda i: (0, 0), pipeline_mode=pl.Buffered(1)),
            pl.BlockSpec((1, sc), lambda i: (0, 0)),
        ],
        out_specs=pl.BlockSpec((tm, n), lambda i: (i, 0)),
        out_shape=jax.ShapeDtypeStruct((t, n), F32),
        scratch_shapes=[pltpu.VMEM((1, sc), F32), pltpu.VMEM((d, n), BF16)],
        compiler_params=pltpu.CompilerParams(
            dimension_semantics=("arbitrary",), vmem_limit_bytes=VMEM_LIMIT_BYTES),
        name="inproj",
    )(x2d, g, w, mu)


def _unit_lower_inverses(n_mats):
    c = n_mats[0].shape[0]
    ri = lax.broadcasted_iota(jnp.int32, (c, c), 0)
    ci = lax.broadcasted_iota(jnp.int32, (c, c), 1)
    same = lambda size: (ri >> (size.bit_length() - 1)) == (ci >> (size.bit_length() - 1))
    eye = jnp.where(ri == ci, 1.0, 0.0).astype(F32)
    base = INVERSE_BASE_BLOCK
    p = [jnp.where(same(base), n, 0.0) for n in n_mats]
    t = [eye + n for n in p]
    for _ in range(base.bit_length() - 2):
        p = [_dot(x, x) for x in p]
        t = [x + _dot(x, y) for x, y in zip(t, p)]
    size = base
    while size < CHUNK:
        off = [jnp.where(same(2 * size) & jnp.logical_not(same(size)), n, 0.0) for n in n_mats]
        q = [_dot(x, n) for x, n in zip(t, off)]
        t = [x + _dot(y, x) for x, y in zip(t, q)]
        size *= 2
    return t


def _rwkv_kernel(p_ref, w0_ref, w2_ref, a0_ref, a2_ref, g2_ref, kk_ref, ka_ref, rk_ref,
                 lnw_ref, lnb_ref, *rest):
    n_cast = (len(rest) - 2) // 2
    o_ref, state_ref = rest[n_cast], rest[-1]
    for src, dst in zip(rest[:n_cast], rest[n_cast + 1:-1]):
        dst[...] = src[...].astype(BF16)

    c = CHUNK
    nrows = p_ref.shape[0]
    nchunks = p_ref.shape[1] // c
    w = w0_ref.shape[-1]
    n_pairs = w // LANES
    rc = nrows * c
    assert c & (c - 1) == 0 and c >= INVERSE_BASE_BLOCK
    log2_c = c.bit_length() - 1

    @pl.when(pl.program_id(1) == 0)
    def _():
        state_ref[...] = jnp.zeros_like(state_ref)

    first_head = lambda n: lax.broadcasted_iota(jnp.int32, (n, LANES), 1) < HEAD_DIM
    head0 = first_head(c)

    def head_sum(x):
        h0 = first_head(x.shape[0])
        s0 = jnp.sum(jnp.where(h0, x, 0.0), axis=-1, keepdims=True)
        s1 = jnp.sum(jnp.where(h0, 0.0, x), axis=-1, keepdims=True)
        return jnp.where(h0, s0, s1)

    def stack(x):
        return jnp.concatenate([jnp.where(head0, x, 0.0), jnp.where(head0, 0.0, x)], axis=0)

    ti = lax.broadcasted_iota(jnp.int32, (rc, rc), 0)
    si = lax.broadcasted_iota(jnp.int32, (rc, rc), 1)
    tri = jnp.where((si <= ti) & ((si >> log2_c) == (ti >> log2_c)), 1.0, 0.0).astype(BF16)
    t2 = lax.broadcasted_iota(jnp.int32, (2 * c, 2 * c), 0) & (c - 1)
    s2 = lax.broadcasted_iota(jnp.int32, (2 * c, 2 * c), 1) & (c - 1)
    is_r_row = lax.broadcasted_iota(jnp.int32, (2 * c, 2 * c), 0) >= c
    causal = (s2 < t2) | (is_r_row & (s2 == t2))

    units = [(b, j) for b in range(nrows) for j in range(n_pairs)]
    n_u = range(len(units))
    rows = lambda b: slice(b * c, (b + 1) * c)
    lanes = lambda j: slice(j * LANES, (j + 1) * LANES)

    def operand_stage(ch):
        cols = lambda lo, hi: jnp.concatenate(
            [p_ref[b, ch * c:(ch + 1) * c, lo:hi] for b in range(nrows)], axis=0)
        r = cols(0, w)
        k = cols(w, 2 * w)
        v = cols(2 * w, 3 * w)
        xwa = cols(3 * w, 3 * w + LANES)
        xg = cols(3 * w + LANES, 3 * w + 2 * LANES)

        z = w0_ref[...] + _dot(jnp.tanh(xwa), w2_ref[...])
        ld = -DECAY_SCALE * _sigmoid(z)
        a = _sigmoid(a0_ref[...] + _dot(xwa, a2_ref[...]))
        g = _dot(_sigmoid(xg), g2_ref[...])

        ld_hi = ld.astype(BF16)
        ld_lo = (ld - ld_hi.astype(F32)).astype(BF16)
        l_incl = _dot(tri, ld_hi) + _dot(tri, ld_lo)
        e_incl = jnp.exp(l_incl)
        e_excl = jnp.exp(l_incl - ld)
        e_inv = jnp.exp(-l_incl)

        kk_all = k * kk_ref[...]
        k2_all = k * (1.0 + (a - 1.0) * ka_ref[...])
        rk_all = r * k2_all * rk_ref[...]
        at_all, bt_all = [], []
        for j in range(n_pairs):
            sl = lanes(j)
            kk = kk_all[:, sl]
            kk = kk * lax.rsqrt(jnp.maximum(head_sum(kk * kk), KK_NORM_FLOOR ** 2))
            at_all.append(-kk * e_excl[:, sl])
            bt_all.append(kk * a[:, sl] * e_inv[:, sl])
        rt_all = r * e_incl
        kt_all = k2_all * e_inv

        ar_s = [jnp.concatenate([stack(at_all[j][rows(b)]), stack(rt_all[rows(b), lanes(j)])],
                                axis=0).astype(BF16) for b, j in units]
        bk_s = [jnp.concatenate([stack(bt_all[j][rows(b)]), stack(kt_all[rows(b), lanes(j)])],
                                axis=0) for b, j in units]
        v_s = [stack(v[rows(b), lanes(j)]) for b, j in units]
        wc = [e_incl[b * c + c - 1:(b + 1) * c, lanes(j)] for b, j in units]
        bonus = [head_sum(rk_all[rows(b), lanes(j)]) * v[rows(b), lanes(j)] for b, j in units]
        gate = [g[rows(b), lanes(j)] for b, j in units]
        return ar_s, bk_s, v_s, wc, bonus, gate

    def state_free_stage(operands):
        ar_s, bk_s, v_s, wc, bonus, gate = operands
        head_rows = lambda x, h, flip=False: jnp.concatenate(
            [x[h * c:(h + 1) * c], x[2 * c + h * c:2 * c + (h + 1) * c]][::-1 if flip else 1], axis=0)
        m = [[jnp.where(causal, _dot(head_rows(ar_s[i], h), head_rows(bk_s[i], h, flip=h == 1), _NT), 0.0)
              for h in range(2)] for i in n_u]
        n_ab = [jnp.concatenate([jnp.where(head0, m0[:c], 0.0), jnp.where(head0, 0.0, m1[:c])], axis=0)
                for m0, m1 in m]
        t_inv = _unit_lower_inverses(n_ab)
        zeros = jnp.zeros((c, LANES), F32)
        akv = [jnp.concatenate([_dot(m[i][0][:c], jnp.concatenate([zeros, v_s[i][:c]], axis=0)),
                                _dot(m[i][1][:c], jnp.concatenate([v_s[i][c:], zeros], axis=0))], axis=0)
               for i in n_u]
        m_r = [[mh[c:] for mh in m[i]] for i in n_u]
        return ar_s, bk_s, v_s, wc, t_inv, akv, m_r, bonus, gate

    def state_stage(ch, hs, ops):
        ar_s, bk_s, v_s, wc, t_inv, akv, m_r, bonus, gate = ops
        arh = [_dot(ar_s[i], hs[i], _NT) for i in n_u]
        u = [_dot(t_inv[i], arh[i][:2 * c] + akv[i]) for i in n_u]
        uv = [jnp.concatenate([u[i], v_s[i]], axis=0) for i in n_u]
        hs_new = [(hs[i] + _dot(uv[i], bk_s[i], _TN)) * wc[i] for i in n_u]
        y_h = [[arh[i][(2 + h) * c:(3 + h) * c]
                + _dot(m_r[i][h], jnp.concatenate(
                    [u[i][h * c:(h + 1) * c], v_s[i][h * c:(h + 1) * c]][::-1 if h else 1], axis=0))
                for h in range(2)] for i in n_u]
        for i, (b, j) in enumerate(units):
            sl = lanes(j)
            y = y_h[i][0] + y_h[i][1]
            mean = head_sum(y) * (1.0 / HEAD_DIM)
            yc = y - mean
            var = head_sum(yc * yc) * (1.0 / HEAD_DIM)
            yn = yc * lax.rsqrt(var + GN_EPS) * lnw_ref[:, sl] + lnb_ref[:, sl]
            o_ref[b, ch * c:(ch + 1) * c, sl] = (yn + bonus[i]) * gate[i]
        return hs_new

    operands = [operand_stage(ch) for ch in range(nchunks)]
    ops = [state_free_stage(x) for x in operands]
    hs = [state_ref[i] for i in n_u]
    for ch in range(nchunks):
        hs = state_stage(ch, hs, ops[ch])
    for i in n_u:
        state_ref[i] = hs[i]


def _rwkv(proj, w0, w2p, a0, a2p, g2, k_k, k_a, r_k, ln_w, ln_b, nrows, nchunks, to_bf16):
    b, s, _ = proj.shape
    w = w0.shape[-1]
    shift_cols = 3 * w + 2 * LANES
    c = CHUNK * nchunks
    grid = (b // nrows, s // c)
    steps = grid[0] * grid[1]
    row = lambda n: pl.BlockSpec((1, n), lambda bi, ci: (0, 0))
    mat = lambda m, n: pl.BlockSpec((m, n), lambda bi, ci: (0, 0))

    def cast_spec(x):
        parts = steps
        while x.shape[0] % (parts * BF16_SUBLANES):
            parts //= 2
        assert parts >= 1 and steps % parts == 0
        return pl.BlockSpec((x.shape[0] // parts, x.shape[1]),
                            lambda bi, ci: (jnp.minimum(bi * grid[1] + ci, parts - 1), 0))

    cast_specs = [cast_spec(x) for x in to_bf16]
    out = pl.pallas_call(
        _rwkv_kernel,
        grid=grid,
        in_specs=[
            pl.BlockSpec((nrows, c, shift_cols), lambda bi, ci: (bi, ci, 0)),
            row(w), mat(LANES, w), row(w), mat(LANES, w), mat(GATE_LORA, w),
            row(w), row(w), row(w), row(w), row(w),
        ] + cast_specs,
        out_specs=[pl.BlockSpec((nrows, c, w), lambda bi, ci: (bi, ci, 0))] + cast_specs,
        out_shape=[jax.ShapeDtypeStruct((b, s, w), F32)]
                  + [jax.ShapeDtypeStruct(x.shape, BF16) for x in to_bf16],
        scratch_shapes=[pltpu.VMEM((nrows * (w // LANES), LANES, LANES), F32)],
        compiler_params=pltpu.CompilerParams(
            dimension_semantics=("arbitrary", "arbitrary"), vmem_limit_bytes=VMEM_LIMIT_BYTES),
        name="rwkv7",
    )(proj, w0, w2p, a0, a2p, g2, k_k, k_a, r_k, ln_w, ln_b, *to_bf16)
    return out[0], out[1:]


def _attn_kernel(q_ref, k_ref, v_ref, g_ref, o_ref, ob_ref, mx_ref, den_ref):
    s_len = q_ref.shape[0]
    blk = ATTN_BLOCK
    scale = HEAD_DIM ** -0.5
    lane = lax.broadcasted_iota(jnp.int32, (blk, LANES), 1)
    head0 = lane < HEAD_DIM

    def stack(x):
        return jnp.concatenate([jnp.where(head0, x, 0.0), jnp.where(head0, 0.0, x)], axis=0)

    def window_bias(span, has_prev):
        nk = 2 * blk if has_prev else blk
        qi = lax.broadcasted_iota(jnp.int32, (2 * blk, nk), 0) & (blk - 1)
        kj = lax.broadcasted_iota(jnp.int32, (2 * blk, nk), 1)
        rel = qi - kj + (blk if has_prev else 0)
        return jnp.where((rel >= 0) & (rel <= span), 0.0, -jnp.inf).astype(F32)

    def scores_stage(bi, dil, bias, blocks):
        ds = lambda s: pl.ds(s, blk, stride=dil) if dil > 1 else pl.ds(s, blk)
        rows = [ds(start) for start, _ in blocks]
        q_s = [stack(q_ref[r, :] * (scale * LOG2_E)).astype(BF16) for r in rows]
        kcat = [k_ref[r, :].astype(BF16) if ps is None else
                jnp.concatenate([k_ref[ds(ps), :], k_ref[r, :]], axis=0).astype(BF16)
                for r, (_, ps) in zip(rows, blocks)]
        vcat = [v_ref[r, :].astype(BF16) if ps is None else
                jnp.concatenate([v_ref[ds(ps), :], v_ref[r, :]], axis=0).astype(BF16)
                for r, (_, ps) in zip(rows, blocks)]
        sc = [_dot(q, kc, _NT) + bias[ps is not None]
              for q, kc, (_, ps) in zip(q_s, kcat, blocks)]
        return bi, rows, sc, vcat

    def softmax_stage(bi, rows, sc, vcat):
        mx = [jnp.max(x, axis=-1, keepdims=True) for x in sc]
        pe = [jnp.exp2(x - m).astype(BF16) for x, m in zip(sc, mx)]
        return bi, rows, pe, mx, vcat

    def pv_stage(pe, mx, vcat):
        ones = jnp.ones((2 * blk, LANES), BF16)
        ov = [_dot(x, jnp.concatenate([vc, ones[:vc.shape[0]]], axis=1)) for x, vc in zip(pe, vcat)]
        pair = lambda x: jnp.where(head0, x[:blk], x[blk:])
        return [(pair(x[:, :LANES]), pair(m), pair(x[:, LANES:])) for x, m in zip(ov, mx)]

    add = lambda xs: functools.reduce(lambda x, y: x + y, xs)

    def store_stage(slot, rows, res):
        for r, (o, m, d) in zip(rows, res):
            ob_ref[slot, r, :] = o
            mx_ref[slot, r, :] = m
            den_ref[slot, r, :] = d

    def mix_stage(rows, res):
        slots = range(ob_ref.shape[0])
        for r, (o_here, m_here, d_here) in zip(rows, res):
            ms_ = [m_here] + [mx_ref[s, r, :] for s in slots]
            os_ = [o_here] + [ob_ref[s, r, :] for s in slots]
            ds_ = [d_here] + [den_ref[s, r, :] for s in slots]
            top = functools.reduce(jnp.maximum, ms_)
            es = [jnp.exp2(m - top) for m in ms_]
            o = add([e * x for e, x in zip(es, os_)]) / add([e * d for e, d in zip(es, ds_)])
            o2 = o * o
            s0 = jnp.sum(jnp.where(head0, o2, 0.0), axis=-1, keepdims=True)
            s1 = jnp.sum(jnp.where(head0, 0.0, o2), axis=-1, keepdims=True)
            ms = jnp.where(head0, s0, s1) * (1.0 / HEAD_DIM)
            o_ref[r, :] = o * lax.rsqrt(ms + NORM_EPS) * g_ref[...]

    group = ATTN_GROUP
    groups = []
    order = sorted(range(len(DILATED_PAIRS)), key=lambda b: -DILATED_PAIRS[b][1])
    assert DILATED_PAIRS[order[-1]][1] == 1
    for bi in order:
        window, dil = DILATED_PAIRS[bi]
        sub_len = s_len // dil
        span = window // dil
        assert sub_len % blk == 0
        nb = sub_len // blk
        blocks = [(n * blk * dil + r, (n - 1) * blk * dil + r if n else None)
                  for r in range(dil) for n in range(nb)]
        assert len(blocks) % group == 0
        groups += [(bi, dil, span, nb, blocks[i:i + group]) for i in range(0, len(blocks), group)]

    def finish(bi, rows, pe, mx, vcat):
        res = pv_stage(pe, mx, vcat)
        if bi == order[-1]:
            mix_stage(rows, res)
        else:
            store_stage(order.index(bi), rows, res)

    biases = {}
    scored, soft = None, None
    for bi, dil, span, nb, blocks in groups + [(None,) * 5] * 2:
        nxt = None
        if blocks is not None:
            if (span, nb) not in biases:
                biases[span, nb] = {False: window_bias(span, False),
                                    True: window_bias(span, True) if nb > 1 else None}
            nxt = scores_stage(bi, dil, biases[span, nb], blocks)
        nxt_soft = softmax_stage(*scored) if scored is not None else None
        if soft is not None:
            finish(*soft)
        scored, soft = nxt, nxt_soft


def _attention(proj, out_g, q_col0, width):
    b, s, _ = proj.shape
    n_pairs = width // LANES
    qb = q_col0 // LANES
    spec = lambda off: pl.BlockSpec((None, s, LANES), lambda bi, pi: (bi, 0, off + pi))
    return pl.pallas_call(
        _attn_kernel,
        grid=(b, n_pairs),
        in_specs=[spec(qb), spec(qb + n_pairs), spec(qb + 2 * n_pairs),
                  pl.BlockSpec((1, LANES), lambda bi, pi: (0, pi))],
        out_specs=pl.BlockSpec((None, s, LANES), lambda bi, pi: (bi, 0, pi)),
        out_shape=jax.ShapeDtypeStruct((b, s, width), F32),
        scratch_shapes=[
            pltpu.VMEM((len(DILATED_PAIRS) - 1, s, LANES), F32) for _ in range(3)],
        compiler_params=pltpu.CompilerParams(
            dimension_semantics=("arbitrary", "arbitrary"), vmem_limit_bytes=VMEM_LIMIT_BYTES),
        name="dilated_attn",
    )(proj, proj, proj, out_g)


def _ffn_kernel(x_ref, ya_ref, yb_ref, wo_ref, gn_ref, wg_ref, wu_ref, wd_ref, gf_ref, o_ref, *, final):
    wa = ya_ref.shape[-1]
    x1 = (x_ref[...]
          + jnp.dot(ya_ref[...].astype(BF16), wo_ref[0:wa, :], preferred_element_type=F32)
          + jnp.dot(yb_ref[...].astype(BF16), wo_ref[wa:, :], preferred_element_type=F32))
    h = _rmsnorm(x1, gn_ref[...]).astype(BF16)
    gate = jnp.dot(h, wg_ref[...], preferred_element_type=F32)
    up = jnp.dot(h, wu_ref[...], preferred_element_type=F32)
    act = (gate * _sigmoid(gate)) * up
    x2 = x1 + jnp.dot(act.astype(BF16), wd_ref[...], preferred_element_type=F32)
    o_ref[...] = _rmsnorm(x2, gf_ref[...]) if final else x2


def _ffn(x2d, ya, yb, wo, gn, wg, wu, wd, gf, tm, final):
    t, d = x2d.shape
    wa, wb = ya.shape[-1], yb.shape[-1]
    dff = wg.shape[-1]
    const = lambda m, n: pl.BlockSpec((m, n), lambda i: (0, 0), pipeline_mode=pl.Buffered(1))
    return pl.pallas_call(
        functools.partial(_ffn_kernel, final=final),
        grid=(t // tm,),
        in_specs=[
            pl.BlockSpec((tm, d), lambda i: (i, 0)),
            pl.BlockSpec((tm, wa), lambda i: (i, 0)),
            pl.BlockSpec((tm, wb), lambda i: (i, 0)),
            const(wa + wb, d), const(1, d), const(d, dff), const(d, dff), const(dff, d), const(1, d),
        ],
        out_specs=pl.BlockSpec((tm, d), lambda i: (i, 0)),
        out_shape=jax.ShapeDtypeStruct((t, d), F32),
        compiler_params=pltpu.CompilerParams(
            dimension_semantics=("arbitrary",), vmem_limit_bytes=VMEM_LIMIT_BYTES),
        name="outproj_ffn",
    )(x2d, ya, yb, wo, gn, wg, wu, wd, gf)


def kernel(x, mix_norm_g, w_in, mu_shift, decay_w0, decay_w2, iclr_a0, iclr_a2, gate_g2, k_k, k_a, r_k,
           ln_x_w, ln_x_b, attn_out_g, w_out, ffn_norm_g, w_gate, w_up, w_down, final_norm_g):
    b, s, d = x.shape
    depth = w_in.shape[0]
    w = decay_w0.shape[-1]
    shift_cols = mu_shift.shape[-1]
    attn_w = attn_out_g.shape[-1]
    assert shift_cols == 3 * w + DECAY_LORA + AAA_LORA + GATE_LORA
    assert DECAY_LORA + AAA_LORA == LANES and shift_cols % LANES == 0
    tm = TOKEN_TILE
    rows_per_step = RWKV_ROWS_PER_STEP if b % RWKV_ROWS_PER_STEP == 0 else 1
    x2d = x.reshape(b * s, d)
    for i in range(depth):
        proj = _inproj(x2d, mix_norm_g[i][None], w_in[i], mu_shift[i][None], tm, s).reshape(b, s, -1)
        zeros = jnp.zeros((LANES - DECAY_LORA, w), F32)
        w2p = jnp.concatenate([decay_w2[i], zeros], axis=0)
        a2p = jnp.concatenate([zeros, iclr_a2[i]], axis=0)
        y_a, (wo, wg, wu, wd) = _rwkv(
            proj, decay_w0[i][None], w2p, iclr_a0[i][None], a2p, gate_g2[i],
            k_k[i][None], k_a[i][None], r_k[i].reshape(1, w), ln_x_w[i][None], ln_x_b[i][None],
            nrows=rows_per_step, nchunks=RWKV_CHUNKS_PER_STEP,
            to_bf16=(w_out[i], w_gate[i], w_up[i], w_down[i]))
        y_b = _attention(proj, attn_out_g[i][None], shift_cols, attn_w)
        x2d = _ffn(x2d, y_a.reshape(b * s, w), y_b.reshape(b * s, attn_w), wo, ffn_norm_g[i][None],
                   wg, wu, wd, final_norm_g[None], tm, final=(i == depth - 1))
    return x2d.reshape(b, s, d)
```

```python
import functools

import jax
import jax.numpy as jnp
from jax import lax
from jax.experimental import pallas as pl
from jax.experimental.pallas import tpu as pltpu

HEAD_DIM = 64
LANES = 128
BF16_SUBLANES = 16
DECAY_LORA = 64
AAA_LORA = 64
GATE_LORA = 128
DILATED_PAIRS = ((128, 1), (512, 4), (2048, 16))
ATTN_BLOCK = 128
NORM_EPS = 1e-6
GN_EPS = 64e-5
CHUNK = 64
INVERSE_BASE_BLOCK = 16
KK_NORM_FLOOR = 1e-12
TOKEN_TILE = 512
RWKV_ROWS_PER_STEP = 4
RWKV_CHUNKS_PER_STEP = 2
ATTN_GROUP = 4
ATTN_PERM = 4
DECAY_SCALE = 0.6065306597126334
LOG2_E = 1.4426950408889634
VMEM_LIMIT_BYTES = 56 * 1024 * 1024
F32 = jnp.float32
BF16 = jnp.bfloat16


def _dot(a, b, dims=((1,), (0,))):
    return lax.dot_general(a.astype(BF16), b.astype(BF16), (dims, ((), ())),
                           preferred_element_type=F32)


_NT = ((1,), (1,))
_TN = ((0,), (0,))


def _rmsnorm(x, g):
    return x * lax.rsqrt(jnp.mean(x * x, axis=-1, keepdims=True) + NORM_EPS) * g


def _sigmoid(x):
    return 0.5 * jnp.tanh(0.5 * x) + 0.5


def _inproj_kernel(x_ref, g_ref, w_ref, mu_ref, o_ref, carry_ref, wb_ref, *, tiles_per_seq):
    tm = x_ref.shape[0]
    sc = mu_ref.shape[-1]

    @pl.when(pl.program_id(0) == 0)
    def _():
        for lo in range(0, w_ref.shape[1], LANES * 2):
            wb_ref[:, lo:lo + LANES * 2] = w_ref[:, lo:lo + LANES * 2].astype(BF16)

    @pl.when(pl.program_id(0) % tiles_per_seq == 0)
    def _():
        carry_ref[...] = jnp.zeros_like(carry_ref)

    h = _rmsnorm(x_ref[...], g_ref[...]).astype(BF16)
    p = jnp.dot(h, wb_ref[:, :sc], preferred_element_type=F32)
    first = lax.broadcasted_iota(jnp.int32, (tm, sc), 0) == 0
    prev = jnp.where(first, carry_ref[...], pltpu.roll(p, 1, axis=0))
    carry_ref[...] = p[tm - 1:tm, :]
    o_ref[:, :sc] = p + (prev - p) * mu_ref[...]
    o_ref[:, sc:] = jnp.dot(h, wb_ref[:, sc:], preferred_element_type=F32)


def _inproj(x2d, g, w, mu, tm, seq_len):
    t, d = x2d.shape
    n = w.shape[1]
    sc = mu.shape[-1]
    assert seq_len % tm == 0 and sc % LANES == 0 and n % (2 * LANES) == 0
    return pl.pallas_call(
        functools.partial(_inproj_kernel, tiles_per_seq=seq_len // tm),
        grid=(t // tm,),
        in_specs=[
            pl.BlockSpec((tm, d), lambda i: (i, 0)),
            pl.BlockSpec((1, d), lambda i: (0, 0)),
            pl.BlockSpec((d, n), lambda i: (0, 0), pipeline_mode=pl.Buffered(1)),
            pl.BlockSpec((1, sc), lambda i: (0, 0)),
        ],
        out_specs=pl.BlockSpec((tm, n), lambda i: (i, 0)),
        out_shape=jax.ShapeDtypeStruct((t, n), F32),
        scratch_shapes=[pltpu.VMEM((1, sc), F32), pltpu.VMEM((d, n), BF16)],
        compiler_params=pltpu.CompilerParams(
            dimension_semantics=("arbitrary",), vmem_limit_bytes=VMEM_LIMIT_BYTES),
        name="inproj",
    )(x2d, g, w, mu)


def _unit_lower_inverses(n_mats):
    c = n_mats[0].shape[0]
    ri = lax.broadcasted_iota(jnp.int32, (c, c), 0)
    ci = lax.broadcasted_iota(jnp.int32, (c, c), 1)
    same = lambda size: (ri >> (size.bit_length() - 1)) == (ci >> (size.bit_length() - 1))
    eye = jnp.where(ri == ci, 1.0, 0.0).astype(F32)
    base = INVERSE_BASE_BLOCK
    p = [jnp.where(same(base), n, 0.0) for n in n_mats]
    t = [eye + n for n in p]
    for _ in range(base.bit_length() - 2):
        p = [_dot(x, x) for x in p]
        t = [x + _dot(x, y) for x, y in zip(t, p)]
    size = base
    while size < CHUNK:
        off = [jnp.where(same(2 * size) & jnp.logical_not(same(size)), n, 0.0) for n in n_mats]
        q = [_dot(x, n) for x, n in zip(t, off)]
        t = [x + _dot(y, x) for x, y in zip(t, q)]
        size *= 2
    return t


def _rwkv_kernel(p_ref, w0_ref, w2_ref, a0_ref, a2_ref, g2_ref, kk_ref, ka_ref, rk_ref,
                 lnw_ref, lnb_ref, *rest):
    n_cast = (len(rest) - 2) // 2
    o_ref, state_ref = rest[n_cast], rest[-1]
    for src, dst in zip(rest[:n_cast], rest[n_cast + 1:-1]):
        dst[...] = src[...].astype(BF16)

    c = CHUNK
    nrows = p_ref.shape[0]
    nchunks = p_ref.shape[1] // c
    w = w0_ref.shape[-1]
    n_pairs = w // LANES
    rc = nrows * c
    assert c & (c - 1) == 0 and c >= INVERSE_BASE_BLOCK
    log2_c = c.bit_length() - 1

    @pl.when(pl.program_id(1) == 0)
    def _():
        state_ref[...] = jnp.zeros_like(state_ref)

    first_head = lambda n: lax.broadcasted_iota(jnp.int32, (n, LANES), 1) < HEAD_DIM
    head0 = first_head(c)

    def head_sum(x):
        h0 = first_head(x.shape[0])
        s0 = jnp.sum(jnp.where(h0, x, 0.0), axis=-1, keepdims=True)
        s1 = jnp.sum(jnp.where(h0, 0.0, x), axis=-1, keepdims=True)
        return jnp.where(h0, s0, s1)

    def stack(x):
        return jnp.concatenate([jnp.where(head0, x, 0.0), jnp.where(head0, 0.0, x)], axis=0)

    ti = lax.broadcasted_iota(jnp.int32, (rc, rc), 0)
    si = lax.broadcasted_iota(jnp.int32, (rc, rc), 1)
    tri = jnp.where((si <= ti) & ((si >> log2_c) == (ti >> log2_c)), 1.0, 0.0).astype(BF16)
    t4 = lax.broadcasted_iota(jnp.int32, (4 * c, 4 * c), 0) & (c - 1)
    s4 = lax.broadcasted_iota(jnp.int32, (4 * c, 4 * c), 1) & (c - 1)
    is_r_row = lax.broadcasted_iota(jnp.int32, (4 * c, 4 * c), 0) >= 2 * c
    causal = (s4 < t4) | (is_r_row & (s4 == t4))

    units = [(b, j) for b in range(nrows) for j in range(n_pairs)]
    n_u = range(len(units))
    rows = lambda b: slice(b * c, (b + 1) * c)
    lanes = lambda j: slice(j * LANES, (j + 1) * LANES)

    def operand_stage(ch):
        cols = lambda lo, hi: jnp.concatenate(
            [p_ref[b, ch * c:(ch + 1) * c, lo:hi] for b in range(nrows)], axis=0)
        r = cols(0, w)
        k = cols(w, 2 * w)
        v = cols(2 * w, 3 * w)
        xwa = cols(3 * w, 3 * w + LANES)
        xg = cols(3 * w + LANES, 3 * w + 2 * LANES)

        z = w0_ref[...] + _dot(jnp.tanh(xwa), w2_ref[...])
        ld = -DECAY_SCALE * _sigmoid(z)
        a = _sigmoid(a0_ref[...] + _dot(xwa, a2_ref[...]))
        g = _dot(_sigmoid(xg), g2_ref[...])

        ld_hi = ld.astype(BF16)
        ld_lo = (ld - ld_hi.astype(F32)).astype(BF16)
        l_incl = _dot(tri, ld_hi) + _dot(tri, ld_lo)
        e_incl = jnp.exp(l_incl)
        e_excl = jnp.exp(l_incl - ld)
        e_inv = jnp.exp(-l_incl)

        kk_all = k * kk_ref[...]
        k2_all = k * (1.0 + (a - 1.0) * ka_ref[...])
        rk_all = r * k2_all * rk_ref[...]
        at_all, bt_all = [], []
        for j in range(n_pairs):
            sl = lanes(j)
            kk = kk_all[:, sl]
            kk = kk * lax.rsqrt(jnp.maximum(head_sum(kk * kk), KK_NORM_FLOOR ** 2))
            at_all.append(-kk * e_excl[:, sl])
            bt_all.append(kk * a[:, sl] * e_inv[:, sl])
        rt_all = r * e_incl
        kt_all = k2_all * e_inv

        ar_s = [jnp.concatenate([stack(at_all[j][rows(b)]), stack(rt_all[rows(b), lanes(j)])],
                                axis=0).astype(BF16) for b, j in units]
        bk_s = [jnp.concatenate([stack(bt_all[j][rows(b)]), stack(kt_all[rows(b), lanes(j)])],
                                axis=0) for b, j in units]
        v_s = [stack(v[rows(b), lanes(j)]) for b, j in units]
        wc = [e_incl[b * c + c - 1:(b + 1) * c, lanes(j)] for b, j in units]
        bonus = [head_sum(rk_all[rows(b), lanes(j)]) * v[rows(b), lanes(j)] for b, j in units]
        gate = [g[rows(b), lanes(j)] for b, j in units]
        return ar_s, bk_s, v_s, wc, bonus, gate

    def state_free_stage(operands):
        ar_s, bk_s, v_s, wc, bonus, gate = operands
        m = [jnp.where(causal, _dot(ar_s[i], bk_s[i], _NT), 0.0) for i in n_u]
        t_inv = _unit_lower_inverses([m[i][:2 * c, :2 * c] for i in n_u])
        akv = [_dot(m[i][:2 * c, 2 * c:], v_s[i]) for i in n_u]
        m_r = [m[i][2 * c:, :] for i in n_u]
        return ar_s, bk_s, v_s, wc, t_inv, akv, m_r, bonus, gate

    def state_stage(ch, hs, ops):
        ar_s, bk_s, v_s, wc, t_inv, akv, m_r, bonus, gate = ops
        arh = [_dot(ar_s[i], hs[i], _NT) for i in n_u]
        u = [_dot(t_inv[i], arh[i][:2 * c] + akv[i]) for i in n_u]
        uv = [jnp.concatenate([u[i], v_s[i]], axis=0) for i in n_u]
        hs_new = [(hs[i] + _dot(uv[i], bk_s[i], _TN)) * wc[i] for i in n_u]
        y_s = [arh[i][2 * c:] + _dot(m_r[i], uv[i]) for i in n_u]
        for i, (b, j) in enumerate(units):
            sl = lanes(j)
            y = y_s[i][:c] + y_s[i][c:]
            mean = head_sum(y) * (1.0 / HEAD_DIM)
            yc = y - mean
            var = head_sum(yc * yc) * (1.0 / HEAD_DIM)
            yn = yc * lax.rsqrt(var + GN_EPS) * lnw_ref[:, sl] + lnb_ref[:, sl]
            o_ref[b, ch * c:(ch + 1) * c, sl] = (yn + bonus[i]) * gate[i]
        return hs_new

    operands = [operand_stage(ch) for ch in range(nchunks)]
    ops = [state_free_stage(x) for x in operands]
    hs = [state_ref[i] for i in n_u]
    for ch in range(nchunks):
        hs = state_stage(ch, hs, ops[ch])
    for i in n_u:
        state_ref[i] = hs[i]


def _rwkv(proj, w0, w2p, a0, a2p, g2, k_k, k_a, r_k, ln_w, ln_b, nrows, nchunks, to_bf16):
    b, s, _ = proj.shape
    w = w0.shape[-1]
    shift_cols = 3 * w + 2 * LANES
    c = CHUNK * nchunks
    grid = (b // nrows, s // c)
    steps = grid[0] * grid[1]
    row = lambda n: pl.BlockSpec((1, n), lambda bi, ci: (0, 0))
    mat = lambda m, n: pl.BlockSpec((m, n), lambda bi, ci: (0, 0))

    def cast_spec(x):
        parts = steps
        while x.shape[0] % (parts * BF16_SUBLANES):
            parts //= 2
        assert parts >= 1 and steps % parts == 0
        return pl.BlockSpec((x.shape[0] // parts, x.shape[1]),
                            lambda bi, ci: (jnp.minimum(bi * grid[1] + ci, parts - 1), 0))

    cast_specs = [cast_spec(x) for x in to_bf16]
    out = pl.pallas_call(
        _rwkv_kernel,
        grid=grid,
        in_specs=[
            pl.BlockSpec((nrows, c, shift_cols), lambda bi, ci: (bi, ci, 0)),
            row(w), mat(LANES, w), row(w), mat(LANES, w), mat(GATE_LORA, w),
            row(w), row(w), row(w), row(w), row(w),
        ] + cast_specs,
        out_specs=[pl.BlockSpec((nrows, c, w), lambda bi, ci: (bi, ci, 0))] + cast_specs,
        out_shape=[jax.ShapeDtypeStruct((b, s, w), F32)]
                  + [jax.ShapeDtypeStruct(x.shape, BF16) for x in to_bf16],
        scratch_shapes=[pltpu.VMEM((nrows * (w // LANES), LANES, LANES), F32)],
        compiler_params=pltpu.CompilerParams(
            dimension_semantics=("arbitrary", "arbitrary"), vmem_limit_bytes=VMEM_LIMIT_BYTES),
        name="rwkv7",
    )(proj, w0, w2p, a0, a2p, g2, k_k, k_a, r_k, ln_w, ln_b, *to_bf16)
    return out[0], out[1:]


def _attn_kernel(q_ref, k_ref, v_ref, g_ref, o_ref, ob_ref, mx_ref, den_ref, perm_ref):
    s_len = q_ref.shape[0]
    blk = ATTN_BLOCK
    scale = HEAD_DIM ** -0.5
    lane = lax.broadcasted_iota(jnp.int32, (blk, LANES), 1)
    head0 = lane < HEAD_DIM

    PERM = ATTN_PERM
    quarter = s_len // PERM
    assert all(d == 1 or d % PERM == 0 for _, d in DILATED_PAIRS) and s_len % PERM == 0
    for i, src in enumerate((q_ref, k_ref, v_ref)):
        for b in range(PERM):
            perm_ref[i, b * quarter:(b + 1) * quarter, :] = src[pl.ds(b, quarter, stride=PERM), :]

    def stack(x):
        return jnp.concatenate([jnp.where(head0, x, 0.0), jnp.where(head0, 0.0, x)], axis=0)

    def window_bias(span, has_prev):
        nk = 2 * blk if has_prev else blk
        qi = lax.broadcasted_iota(jnp.int32, (2 * blk, nk), 0) & (blk - 1)
        kj = lax.broadcasted_iota(jnp.int32, (2 * blk, nk), 1)
        rel = qi - kj + (blk if has_prev else 0)
        return jnp.where((rel >= 0) & (rel <= span), 0.0, -jnp.inf).astype(F32)

    def scores_stage(bi, dil, bias, blocks):
        rows = [pl.ds(start, blk, stride=dil) if dil > 1 else pl.ds(start, blk) for start, _ in blocks]
        if dil > 1:
            def ds(s):
                off = (s % PERM) * quarter + s // PERM
                return pl.ds(off, blk, stride=dil // PERM) if dil > PERM else pl.ds(off, blk)
            rd = lambda i, ref, s: perm_ref[i, ds(s), :]
        else:
            rd = lambda i, ref, s: ref[pl.ds(s, blk), :]
        q_s = [stack(rd(0, q_ref, s) * (scale * LOG2_E)).astype(BF16) for s, _ in blocks]
        kcat = [rd(1, k_ref, s).astype(BF16) if ps is None else
                jnp.concatenate([rd(1, k_ref, ps), rd(1, k_ref, s)], axis=0).astype(BF16)
                for s, ps in blocks]
        vcat = [rd(2, v_ref, s).astype(BF16) if ps is None else
                jnp.concatenate([rd(2, v_ref, ps), rd(2, v_ref, s)], axis=0).astype(BF16)
                for s, ps in blocks]
        sc = [_dot(q, kc, _NT) + bias[ps is not None]
              for q, kc, (_, ps) in zip(q_s, kcat, blocks)]
        return bi, rows, sc, vcat

    def softmax_stage(bi, rows, sc, vcat):
        mx = [jnp.max(x, axis=-1, keepdims=True) for x in sc]
        pe = [jnp.exp2(x - m).astype(BF16) for x, m in zip(sc, mx)]
        return bi, rows, pe, mx, vcat

    def pv_stage(pe, mx, vcat):
        ones = jnp.ones((2 * blk, LANES), BF16)
        ov = [_dot(x, jnp.concatenate([vc, ones[:vc.shape[0]]], axis=1)) for x, vc in zip(pe, vcat)]
        pair = lambda x: jnp.where(head0, x[:blk], x[blk:])
        return [(pair(x[:, :LANES]), pair(m), pair(x[:, LANES:])) for x, m in zip(ov, mx)]

    add = lambda xs: functools.reduce(lambda x, y: x + y, xs)

    def store_stage(slot, rows, res):
        for r, (o, m, d) in zip(rows, res):
            ob_ref[slot, r, :] = o
            mx_ref[slot, r, :] = m
            den_ref[slot, r, :] = d

    def mix_stage(rows, res):
        slots = range(ob_ref.shape[0])
        for r, (o_here, m_here, d_here) in zip(rows, res):
            ms_ = [m_here] + [mx_ref[s, r, :] for s in slots]
            os_ = [o_here] + [ob_ref[s, r, :] for s in slots]
            ds_ = [d_here] + [den_ref[s, r, :] for s in slots]
            top = functools.reduce(jnp.maximum, ms_)
            es = [jnp.exp2(m - top) for m in ms_]
            o = add([e * x for e, x in zip(es, os_)]) / add([e * d for e, d in zip(es, ds_)])
            o2 = o * o
            s0 = jnp.sum(jnp.where(head0, o2, 0.0), axis=-1, keepdims=True)
            s1 = jnp.sum(jnp.where(head0, 0.0, o2), axis=-1, keepdims=True)
            ms = jnp.where(head0, s0, s1) * (1.0 / HEAD_DIM)
            o_ref[r, :] = o * lax.rsqrt(ms + NORM_EPS) * g_ref[...]

    group = ATTN_GROUP
    groups = []
    order = sorted(range(len(DILATED_PAIRS)), key=lambda b: -DILATED_PAIRS[b][1])
    assert DILATED_PAIRS[order[-1]][1] == 1
    for bi in order:
        window, dil = DILATED_PAIRS[bi]
        sub_len = s_len // dil
        span = window // dil
        assert sub_len % blk == 0
        nb = sub_len // blk
        blocks = [(n * blk * dil + r, (n - 1) * blk * dil + r if n else None)
                  for r in range(dil) for n in range(nb)]
        assert len(blocks) % group == 0
        groups += [(bi, dil, span, nb, blocks[i:i + group]) for i in range(0, len(blocks), group)]

    def finish(bi, rows, pe, mx, vcat):
        res = pv_stage(pe, mx, vcat)
        if bi == order[-1]:
            mix_stage(rows, res)
        else:
            store_stage(order.index(bi), rows, res)

    biases = {}
    scored, soft = None, None
    for bi, dil, span, nb, blocks in groups + [(None,) * 5] * 2:
        nxt = None
        if blocks is not None:
            if (span, nb) not in biases:
                biases[span, nb] = {False: window_bias(span, False),
                                    True: window_bias(span, True) if nb > 1 else None}
            nxt = scores_stage(bi, dil, biases[span, nb], blocks)
        nxt_soft = softmax_stage(*scored) if scored is not None else None
        if soft is not None:
            finish(*soft)
        scored, soft = nxt, nxt_soft


def _attention(proj, out_g, q_col0, width):
    b, s, _ = proj.shape
    n_pairs = width // LANES
    qb = q_col0 // LANES
    spec = lambda off: pl.BlockSpec((None, s, LANES), lambda bi, pi: (bi, 0, off + pi))
    return pl.pallas_call(
        _attn_kernel,
        grid=(b, n_pairs),
        in_specs=[spec(qb), spec(qb + n_pairs), spec(qb + 2 * n_pairs),
                  pl.BlockSpec((1, LANES), lambda bi, pi: (0, pi))],
        out_specs=pl.BlockSpec((None, s, LANES), lambda bi, pi: (bi, 0, pi)),
        out_shape=jax.ShapeDtypeStruct((b, s, width), F32),
        scratch_shapes=[
            pltpu.VMEM((len(DILATED_PAIRS) - 1, s, LANES), F32) for _ in range(3)]
            + [pltpu.VMEM((3, s, LANES), F32)],
        compiler_params=pltpu.CompilerParams(
            dimension_semantics=("arbitrary", "arbitrary"), vmem_limit_bytes=VMEM_LIMIT_BYTES),
        name="dilated_attn",
    )(proj, proj, proj, out_g)


def _ffn_kernel(x_ref, ya_ref, yb_ref, wo_ref, gn_ref, wg_ref, wu_ref, wd_ref, gf_ref, o_ref, *, final):
    wa = ya_ref.shape[-1]
    x1 = (x_ref[...]
          + jnp.dot(ya_ref[...].astype(BF16), wo_ref[0:wa, :], preferred_element_type=F32)
          + jnp.dot(yb_ref[...].astype(BF16), wo_ref[wa:, :], preferred_element_type=F32))
    h = _rmsnorm(x1, gn_ref[...]).astype(BF16)
    gate = jnp.dot(h, wg_ref[...], preferred_element_type=F32)
    up = jnp.dot(h, wu_ref[...], preferred_element_type=F32)
    act = (gate * _sigmoid(gate)) * up
    x2 = x1 + jnp.dot(act.astype(BF16), wd_ref[...], preferred_element_type=F32)
    o_ref[...] = _rmsnorm(x2, gf_ref[...]) if final else x2


def _ffn(x2d, ya, yb, wo, gn, wg, wu, wd, gf, tm, final):
    t, d = x2d.shape
    wa, wb = ya.shape[-1], yb.shape[-1]
    dff = wg.shape[-1]
    const = lambda m, n: pl.BlockSpec((m, n), lambda i: (0, 0), pipeline_mode=pl.Buffered(1))
    return pl.pallas_call(
        functools.partial(_ffn_kernel, final=final),
        grid=(t // tm,),
        in_specs=[
            pl.BlockSpec((tm, d), lambda i: (i, 0)),
            pl.BlockSpec((tm, wa), lambda i: (i, 0)),
            pl.BlockSpec((tm, wb), lambda i: (i, 0)),
            const(wa + wb, d), const(1, d), const(d, dff), const(d, dff), const(dff, d), const(1, d),
        ],
        out_specs=pl.BlockSpec((tm, d), lambda i: (i, 0)),
        out_shape=jax.ShapeDtypeStruct((t, d), F32),
        compiler_params=pltpu.CompilerParams(
            dimension_semantics=("arbitrary",), vmem_limit_bytes=VMEM_LIMIT_BYTES),
        name="outproj_ffn",
    )(x2d, ya, yb, wo, gn, wg, wu, wd, gf)


def kernel(x, mix_norm_g, w_in, mu_shift, decay_w0, decay_w2, iclr_a0, iclr_a2, gate_g2, k_k, k_a, r_k,
           ln_x_w, ln_x_b, attn_out_g, w_out, ffn_norm_g, w_gate, w_up, w_down, final_norm_g):
    b, s, d = x.shape
    depth = w_in.shape[0]
    w = decay_w0.shape[-1]
    shift_cols = mu_shift.shape[-1]
    attn_w = attn_out_g.shape[-1]
    assert shift_cols == 3 * w + DECAY_LORA + AAA_LORA + GATE_LORA
    assert DECAY_LORA + AAA_LORA == LANES and shift_cols % LANES == 0
    tm = TOKEN_TILE
    rows_per_step = RWKV_ROWS_PER_STEP if b % RWKV_ROWS_PER_STEP == 0 else 1
    x2d = x.reshape(b * s, d)
    for i in range(depth):
        proj = _inproj(x2d, mix_norm_g[i][None], w_in[i], mu_shift[i][None], tm, s).reshape(b, s, -1)
        zeros = jnp.zeros((LANES - DECAY_LORA, w), F32)
        w2p = jnp.concatenate([decay_w2[i], zeros], axis=0)
        a2p = jnp.concatenate([zeros, iclr_a2[i]], axis=0)
        y_a, (wo, wg, wu, wd) = _rwkv(
            proj, decay_w0[i][None], w2p, iclr_a0[i][None], a2p, gate_g2[i],
            k_k[i][None], k_a[i][None], r_k[i].reshape(1, w), ln_x_w[i][None], ln_x_b[i][None],
            nrows=rows_per_step, nchunks=RWKV_CHUNKS_PER_STEP,
            to_bf16=(w_out[i], w_gate[i], w_up[i], w_down[i]))
        y_b = _attention(proj, attn_out_g[i][None], shift_cols, attn_w)
        x2d = _ffn(x2d, y_a.reshape(b * s, w), y_b.reshape(b * s, attn_w), wo, ffn_norm_g[i][None],
                   wg, wu, wd, final_norm_g[None], tm, final=(i == depth - 1))
    return x2d.reshape(b, s, d)
```

```python
import functools

import jax
import jax.numpy as jnp
from jax import lax
from jax.experimental import pallas as pl
from jax.experimental.pallas import tpu as pltpu

HEAD_DIM = 64
LANES = 128
BF16_SUBLANES = 16
DECAY_LORA = 64
AAA_LORA = 64
GATE_LORA = 128
DILATED_PAIRS = ((128, 1), (512, 4), (2048, 16))
ATTN_BLOCK = 128
NORM_EPS = 1e-6
GN_EPS = 64e-5
CHUNK = 64
INVERSE_BASE_BLOCK = 16
KK_NORM_FLOOR = 1e-12
TOKEN_TILE = 512
RWKV_ROWS_PER_STEP = 4
RWKV_CHUNKS_PER_STEP = 2
ATTN_GROUP = 4
ATTN_PERM = 4
DECAY_SCALE = 0.6065306597126334
LOG2_E = 1.4426950408889634
VMEM_LIMIT_BYTES = 56 * 1024 * 1024
F32 = jnp.float32
BF16 = jnp.bfloat16


def _dot(a, b, dims=((1,), (0,))):
    return lax.dot_general(a.astype(BF16), b.astype(BF16), (dims, ((), ())),
                           preferred_element_type=F32)


_NT = ((1,), (1,))
_TN = ((0,), (0,))


def _rmsnorm(x, g):
    return x * lax.rsqrt(jnp.mean(x * x, axis=-1, keepdims=True) + NORM_EPS) * g


def _sigmoid(x):
    return 0.5 * jnp.tanh(0.5 * x) + 0.5


def _inproj_kernel(x_ref, g_ref, w_ref, mu_ref, o_ref, carry_ref, wb_ref, *, tiles_per_seq):
    tm = x_ref.shape[0]
    sc = mu_ref.shape[-1]

    @pl.when(pl.program_id(0) == 0)
    def _():
        for lo in range(0, w_ref.shape[1], LANES * 2):
            wb_ref[:, lo:lo + LANES * 2] = w_ref[:, lo:lo + LANES * 2].astype(BF16)

    @pl.when(pl.program_id(0) % tiles_per_seq == 0)
    def _():
        carry_ref[...] = jnp.zeros_like(carry_ref)

    h = _rmsnorm(x_ref[...], g_ref[...]).astype(BF16)
    p = jnp.dot(h, wb_ref[:, :sc], preferred_element_type=F32)
    first = lax.broadcasted_iota(jnp.int32, (tm, sc), 0) == 0
    prev = jnp.where(first, carry_ref[...], pltpu.roll(p, 1, axis=0))
    carry_ref[...] = p[tm - 1:tm, :]
    o_ref[:, :sc] = p + (prev - p) * mu_ref[...]
    o_ref[:, sc:] = jnp.dot(h, wb_ref[:, sc:], preferred_element_type=F32)


def _inproj(x2d, g, w, mu, tm, seq_len):
    t, d = x2d.shape
    n = w.shape[1]
    sc = mu.shape[-1]
    assert seq_len % tm == 0 and sc % LANES == 0 and n % (2 * LANES) == 0
    return pl.pallas_call(
        functools.partial(_inproj_kernel, tiles_per_seq=seq_len // tm),
        grid=(t // tm,),
        in_specs=[
            pl.BlockSpec((tm, d), lambda i: (i, 0)),
            pl.BlockSpec((1, d), lambda i: (0, 0)),
            pl.BlockSpec((d, n), lambda i: (0, 0), pipeline_mode=pl.Buffered(1)),
            pl.BlockSpec((1, sc), lambda i: (0, 0)),
        ],
        out_specs=pl.BlockSpec((tm, n), lambda i: (i, 0)),
        out_shape=jax.ShapeDtypeStruct((t, n), F32),
        scratch_shapes=[pltpu.VMEM((1, sc), F32), pltpu.VMEM((d, n), BF16)],
        compiler_params=pltpu.CompilerParams(
            dimension_semantics=("arbitrary",), vmem_limit_bytes=VMEM_LIMIT_BYTES),
        name="inproj",
    )(x2d, g, w, mu)


def _unit_lower_inverses(n_mats):
    c = n_mats[0].shape[0]
    ri = lax.broadcasted_iota(jnp.int32, (c, c), 0)
    ci = lax.broadcasted_iota(jnp.int32, (c, c), 1)
    same = lambda size: (ri >> (size.bit_length() - 1)) == (ci >> (size.bit_length() - 1))
    eye = jnp.where(ri == ci, 1.0, 0.0).astype(F32)
    base = INVERSE_BASE_BLOCK
    p = [jnp.where(same(base), n, 0.0) for n in n_mats]
    t = [eye + n for n in p]
    for _ in range(base.bit_length() - 2):
        p = [_dot(x, x) for x in p]
        t = [x + _dot(x, y) for x, y in zip(t, p)]
    size = base
    while size < CHUNK:
        off = [jnp.where(same(2 * size) & jnp.logical_not(same(size)), n, 0.0) for n in n_mats]
        q = [_dot(x, n) for x, n in zip(t, off)]
        t = [x + _dot(y, x) for x, y in zip(t, q)]
        size *= 2
    return t


def _rwkv_kernel(p_ref, w0_ref, w2_ref, a0_ref, a2_ref, g2_ref, kk_ref, ka_ref, rk_ref,
                 lnw_ref, lnb_ref, *rest):
    n_cast = (len(rest) - 2) // 2
    o_ref, state_ref = rest[n_cast], rest[-1]
    for src, dst in zip(rest[:n_cast], rest[n_cast + 1:-1]):
        dst[...] = src[...].astype(BF16)

    c = CHUNK
    nrows = p_ref.shape[0]
    nchunks = p_ref.shape[1] // c
    w = w0_ref.shape[-1]
    n_pairs = w // LANES
    rc = nrows * c
    assert c & (c - 1) == 0 and c >= INVERSE_BASE_BLOCK
    log2_c = c.bit_length() - 1

    @pl.when(pl.program_id(1) == 0)
    def _():
        state_ref[...] = jnp.zeros_like(state_ref)

    first_head = lambda n: lax.broadcasted_iota(jnp.int32, (n, LANES), 1) < HEAD_DIM
    head0 = first_head(c)

    def head_sum(x):
        h0 = first_head(x.shape[0])
        s0 = jnp.sum(jnp.where(h0, x, 0.0), axis=-1, keepdims=True)
        s1 = jnp.sum(jnp.where(h0, 0.0, x), axis=-1, keepdims=True)
        return jnp.where(h0, s0, s1)

    def stack(x):
        return jnp.concatenate([jnp.where(head0, x, 0.0), jnp.where(head0, 0.0, x)], axis=0)

    ti = lax.broadcasted_iota(jnp.int32, (rc, rc), 0)
    si = lax.broadcasted_iota(jnp.int32, (rc, rc), 1)
    tri = jnp.where((si <= ti) & ((si >> log2_c) == (ti >> log2_c)), 1.0, 0.0).astype(BF16)
    t4 = lax.broadcasted_iota(jnp.int32, (4 * c, 4 * c), 0) & (c - 1)
    s4 = lax.broadcasted_iota(jnp.int32, (4 * c, 4 * c), 1) & (c - 1)
    is_r_row = lax.broadcasted_iota(jnp.int32, (4 * c, 4 * c), 0) >= 2 * c
    causal = (s4 < t4) | (is_r_row & (s4 == t4))

    units = [(b, j) for b in range(nrows) for j in range(n_pairs)]
    n_u = range(len(units))
    rows = lambda b: slice(b * c, (b + 1) * c)
    lanes = lambda j: slice(j * LANES, (j + 1) * LANES)

    def operand_stage(ch):
        cols = lambda lo, hi: jnp.concatenate(
            [p_ref[b, ch * c:(ch + 1) * c, lo:hi] for b in range(nrows)], axis=0)
        r = cols(0, w)
        k = cols(w, 2 * w)
        v = cols(2 * w, 3 * w)
        xwa = cols(3 * w, 3 * w + LANES)
        xg = cols(3 * w + LANES, 3 * w + 2 * LANES)

        z = w0_ref[...] + _dot(jnp.tanh(xwa), w2_ref[...])
        ld = -DECAY_SCALE * _sigmoid(z)
        a = _sigmoid(a0_ref[...] + _dot(xwa, a2_ref[...]))
        g = _dot(_sigmoid(xg), g2_ref[...])

        ld_hi = ld.astype(BF16)
        ld_lo = (ld - ld_hi.astype(F32)).astype(BF16)
        l_incl = _dot(tri, ld_hi) + _dot(tri, ld_lo)
        e_incl = jnp.exp(l_incl)
        e_excl = jnp.exp(l_incl - ld)
        e_inv = jnp.exp(-l_incl)

        kk_all = k * kk_ref[...]
        k2_all = k * (1.0 + (a - 1.0) * ka_ref[...])
        rk_all = r * k2_all * rk_ref[...]
        at_all, bt_all = [], []
        for j in range(n_pairs):
            sl = lanes(j)
            kk = kk_all[:, sl]
            kk = kk * lax.rsqrt(jnp.maximum(head_sum(kk * kk), KK_NORM_FLOOR ** 2))
            at_all.append(-kk * e_excl[:, sl])
            bt_all.append(kk * a[:, sl] * e_inv[:, sl])
        rt_all = r * e_incl
        kt_all = k2_all * e_inv

        ar_s = [jnp.concatenate([stack(at_all[j][rows(b)]), stack(rt_all[rows(b), lanes(j)])],
                                axis=0).astype(BF16) for b, j in units]
        bk_s = [jnp.concatenate([stack(bt_all[j][rows(b)]), stack(kt_all[rows(b), lanes(j)])],
                                axis=0) for b, j in units]
        v_s = [stack(v[rows(b), lanes(j)]) for b, j in units]
        wc = [e_incl[b * c + c - 1:(b + 1) * c, lanes(j)] for b, j in units]
        bonus = [head_sum(rk_all[rows(b), lanes(j)]) * v[rows(b), lanes(j)] for b, j in units]
        gate = [g[rows(b), lanes(j)] for b, j in units]
        return ar_s, bk_s, v_s, wc, bonus, gate

    def state_free_stage(operands):
        ar_s, bk_s, v_s, wc, bonus, gate = operands
        m = [jnp.where(causal, _dot(ar_s[i], bk_s[i], _NT), 0.0) for i in n_u]
        t_inv = _unit_lower_inverses([m[i][:2 * c, :2 * c] for i in n_u])
        akv = [_dot(m[i][:2 * c, 2 * c:], v_s[i]) for i in n_u]
        m_r = [m[i][2 * c:, :] for i in n_u]
        return ar_s, bk_s, v_s, wc, t_inv, akv, m_r, bonus, gate

    def state_stage(ch, hs, ops):
        ar_s, bk_s, v_s, wc, t_inv, akv, m_r, bonus, gate = ops
        arh = [_dot(ar_s[i], hs[i], _NT) for i in n_u]
        u = [_dot(t_inv[i], arh[i][:2 * c] + akv[i]) for i in n_u]
        uv = [jnp.concatenate([u[i], v_s[i]], axis=0) for i in n_u]
        hs_new = [(hs[i] + _dot(uv[i], bk_s[i], _TN)) * wc[i] for i in n_u]
        y_s = [arh[i][2 * c:] + _dot(m_r[i], uv[i]) for i in n_u]
        for i, (b, j) in enumerate(units):
            sl = lanes(j)
            y = y_s[i][:c] + y_s[i][c:]
            mean = head_sum(y) * (1.0 / HEAD_DIM)
            yc = y - mean
            var = head_sum(yc * yc) * (1.0 / HEAD_DIM)
            yn = yc * lax.rsqrt(var + GN_EPS) * lnw_ref[:, sl] + lnb_ref[:, sl]
            o_ref[b, ch * c:(ch + 1) * c, sl] = (yn + bonus[i]) * gate[i]
        return hs_new

    operands = [operand_stage(ch) for ch in range(nchunks)]
    ops = [state_free_stage(x) for x in operands]
    hs = [state_ref[i] for i in n_u]
    for ch in range(nchunks):
        hs = state_stage(ch, hs, ops[ch])
    for i in n_u:
        state_ref[i] = hs[i]


def _rwkv(proj, w0, w2p, a0, a2p, g2, k_k, k_a, r_k, ln_w, ln_b, nrows, nchunks, to_bf16):
    b, s, _ = proj.shape
    w = w0.shape[-1]
    shift_cols = 3 * w + 2 * LANES
    c = CHUNK * nchunks
    grid = (b // nrows, s // c)
    steps = grid[0] * grid[1]
    row = lambda n: pl.BlockSpec((1, n), lambda bi, ci: (0, 0))
    mat = lambda m, n: pl.BlockSpec((m, n), lambda bi, ci: (0, 0))

    def cast_spec(x):
        parts = steps
        while x.shape[0] % (parts * BF16_SUBLANES):
            parts //= 2
        assert parts >= 1 and steps % parts == 0
        return pl.BlockSpec((x.shape[0] // parts, x.shape[1]),
                            lambda bi, ci: (jnp.minimum(bi * grid[1] + ci, parts - 1), 0))

    cast_specs = [cast_spec(x) for x in to_bf16]
    out = pl.pallas_call(
        _rwkv_kernel,
        grid=grid,
        in_specs=[
            pl.BlockSpec((nrows, c, shift_cols), lambda bi, ci: (bi, ci, 0)),
            row(w), mat(LANES, w), row(w), mat(LANES, w), mat(GATE_LORA, w),
            row(w), row(w), row(w), row(w), row(w),
        ] + cast_specs,
        out_specs=[pl.BlockSpec((nrows, c, w), lambda bi, ci: (bi, ci, 0))] + cast_specs,
        out_shape=[jax.ShapeDtypeStruct((b, s, w), F32)]
                  + [jax.ShapeDtypeStruct(x.shape, BF16) for x in to_bf16],
        scratch_shapes=[pltpu.VMEM((nrows * (w // LANES), LANES, LANES), F32)],
        compiler_params=pltpu.CompilerParams(
            dimension_semantics=("arbitrary", "arbitrary"), vmem_limit_bytes=VMEM_LIMIT_BYTES),
        name="rwkv7",
    )(proj, w0, w2p, a0, a2p, g2, k_k, k_a, r_k, ln_w, ln_b, *to_bf16)
    return out[0], out[1:]


def _attn_kernel(q_ref, k_ref, v_ref, g_ref, o_ref, ob_ref, mx_ref, den_ref, perm_ref):
    s_len = q_ref.shape[0]
    blk = ATTN_BLOCK
    scale = HEAD_DIM ** -0.5
    lane = lax.broadcasted_iota(jnp.int32, (blk, LANES), 1)
    head0 = lane < HEAD_DIM

    PERM = ATTN_PERM
    quarter = s_len // PERM
    run = blk // PERM
    assert run % 8 == 0 and all(d == 1 or d % PERM == 0 for _, d in DILATED_PAIRS) and s_len % PERM == 0
    for i, src in enumerate((q_ref, k_ref, v_ref)):
        for b in range(PERM):
            perm_ref[i, b * quarter:(b + 1) * quarter, :] = src[pl.ds(b, quarter, stride=PERM), :]

    def stack(x):
        return jnp.concatenate([jnp.where(head0, x, 0.0), jnp.where(head0, 0.0, x)], axis=0)

    def segments(s):
        return [pl.ds(b * quarter + s // PERM, run) for b in range(PERM)]

    def window_bias(span, has_prev, perm_q):
        nk = 2 * blk if has_prev else blk
        qi = lax.broadcasted_iota(jnp.int32, (2 * blk, nk), 0) & (blk - 1)
        if perm_q:
            qi = (qi & (run - 1)) * PERM + qi // run
        kj = lax.broadcasted_iota(jnp.int32, (2 * blk, nk), 1)
        rel = qi - kj + (blk if has_prev else 0)
        return jnp.where((rel >= 0) & (rel <= span), 0.0, -jnp.inf).astype(F32)

    def scores_stage(bi, dil, bias, blocks):
        if dil > 1:
            def ds(s):
                off = (s % PERM) * quarter + s // PERM
                return pl.ds(off, blk, stride=dil // PERM) if dil > PERM else pl.ds(off, blk)
            rd = lambda i, ref, s: perm_ref[i, ds(s), :]
            rows = [ds(s) for s, _ in blocks]
            q_in = [rd(0, q_ref, s) for s, _ in blocks]
        else:
            rd = lambda i, ref, s: ref[pl.ds(s, blk), :]
            rows = [s for s, _ in blocks]
            q_in = [jnp.concatenate([perm_ref[0, sg, :] for sg in segments(s)], axis=0) for s in rows]
        q_s = [stack(x * (scale * LOG2_E)).astype(BF16) for x in q_in]
        kcat = [rd(1, k_ref, s).astype(BF16) if ps is None else
                jnp.concatenate([rd(1, k_ref, ps), rd(1, k_ref, s)], axis=0).astype(BF16)
                for s, ps in blocks]
        vcat = [rd(2, v_ref, s).astype(BF16) if ps is None else
                jnp.concatenate([rd(2, v_ref, ps), rd(2, v_ref, s)], axis=0).astype(BF16)
                for s, ps in blocks]
        sc = [_dot(q, kc, _NT) + bias[ps is not None]
              for q, kc, (_, ps) in zip(q_s, kcat, blocks)]
        return bi, rows, sc, vcat

    def softmax_stage(bi, rows, sc, vcat):
        mx = [jnp.max(x, axis=-1, keepdims=True) for x in sc]
        pe = [jnp.exp2(x - m).astype(BF16) for x, m in zip(sc, mx)]
        return bi, rows, pe, mx, vcat

    def pv_stage(pe, mx, vcat):
        ones = jnp.ones((2 * blk, LANES), BF16)
        ov = [_dot(x, jnp.concatenate([vc, ones[:vc.shape[0]]], axis=1)) for x, vc in zip(pe, vcat)]
        pair = lambda x: jnp.where(head0, x[:blk], x[blk:])
        return [(pair(x[:, :LANES]), pair(m), pair(x[:, LANES:])) for x, m in zip(ov, mx)]

    add = lambda xs: functools.reduce(lambda x, y: x + y, xs)

    def store_stage(slot, rows, res):
        for r, (o, m, d) in zip(rows, res):
            ob_ref[slot, r, :] = o
            mx_ref[slot, r, :] = m
            den_ref[slot, r, :] = d

    def mix_stage(rows, res):
        slots = range(ob_ref.shape[0])
        for start, (o_here, m_here, d_here) in zip(rows, res):
            segs = segments(start)
            ld = lambda ref, s: jnp.concatenate([ref[s, sg, :] for sg in segs], axis=0)
            ms_ = [m_here] + [ld(mx_ref, s) for s in slots]
            os_ = [o_here] + [ld(ob_ref, s) for s in slots]
            ds_ = [d_here] + [ld(den_ref, s) for s in slots]
            top = functools.reduce(jnp.maximum, ms_)
            es = [jnp.exp2(m - top) for m in ms_]
            o = add([e * x for e, x in zip(es, os_)]) / add([e * d for e, d in zip(es, ds_)])
            o2 = o * o
            s0 = jnp.sum(jnp.where(head0, o2, 0.0), axis=-1, keepdims=True)
            s1 = jnp.sum(jnp.where(head0, 0.0, o2), axis=-1, keepdims=True)
            ms = jnp.where(head0, s0, s1) * (1.0 / HEAD_DIM)
            out = o * lax.rsqrt(ms + NORM_EPS) * g_ref[...]
            for b in range(PERM):
                o_ref[pl.ds(start + b, run, stride=PERM), :] = out[b * run:(b + 1) * run]

    group = ATTN_GROUP
    groups = []
    order = sorted(range(len(DILATED_PAIRS)), key=lambda b: -DILATED_PAIRS[b][1])
    assert DILATED_PAIRS[order[-1]][1] == 1
    for bi in order:
        window, dil = DILATED_PAIRS[bi]
        sub_len = s_len // dil
        span = window // dil
        assert sub_len % blk == 0
        nb = sub_len // blk
        blocks = [(n * blk * dil + r, (n - 1) * blk * dil + r if n else None)
                  for r in range(dil) for n in range(nb)]
        assert len(blocks) % group == 0
        groups += [(bi, dil, span, nb, blocks[i:i + group]) for i in range(0, len(blocks), group)]

    def finish(bi, rows, pe, mx, vcat):
        res = pv_stage(pe, mx, vcat)
        if bi == order[-1]:
            mix_stage(rows, res)
        else:
            store_stage(order.index(bi), rows, res)

    biases = {}
    scored, soft = None, None
    for bi, dil, span, nb, blocks in groups + [(None,) * 5] * 2:
        nxt = None
        if blocks is not None:
            key = (span, nb, dil == 1)
            if key not in biases:
                biases[key] = {False: window_bias(span, False, dil == 1),
                               True: window_bias(span, True, dil == 1) if nb > 1 else None}
            nxt = scores_stage(bi, dil, biases[key], blocks)
        nxt_soft = softmax_stage(*scored) if scored is not None else None
        if soft is not None:
            finish(*soft)
        scored, soft = nxt, nxt_soft


def _attention(proj, out_g, q_col0, width):
    b, s, _ = proj.shape
    n_pairs = width // LANES
    qb = q_col0 // LANES
    spec = lambda off: pl.BlockSpec((None, s, LANES), lambda bi, pi: (bi, 0, off + pi))
    return pl.pallas_call(
        _attn_kernel,
        grid=(b, n_pairs),
        in_specs=[spec(qb), spec(qb + n_pairs), spec(qb + 2 * n_pairs),
                  pl.BlockSpec((1, LANES), lambda bi, pi: (0, pi))],
        out_specs=pl.BlockSpec((None, s, LANES), lambda bi, pi: (bi, 0, pi)),
        out_shape=jax.ShapeDtypeStruct((b, s, width), F32),
        scratch_shapes=[
            pltpu.VMEM((len(DILATED_PAIRS) - 1, s, LANES), F32) for _ in range(3)]
            + [pltpu.VMEM((3, s, LANES), F32)],
        compiler_params=pltpu.CompilerParams(
            dimension_semantics=("arbitrary", "arbitrary"), vmem_limit_bytes=VMEM_LIMIT_BYTES),
        name="dilated_attn",
    )(proj, proj, proj, out_g)


def _ffn_kernel(x_ref, ya_ref, yb_ref, wo_ref, gn_ref, wg_ref, wu_ref, wd_ref, gf_ref, o_ref, *, final):
    wa = ya_ref.shape[-1]
    x1 = (x_ref[...]
          + jnp.dot(ya_ref[...].astype(BF16), wo_ref[0:wa, :], preferred_element_type=F32)
          + jnp.dot(yb_ref[...].astype(BF16), wo_ref[wa:, :], preferred_element_type=F32))
    h = _rmsnorm(x1, gn_ref[...]).astype(BF16)
    gate = jnp.dot(h, wg_ref[...], preferred_element_type=F32)
    up = jnp.dot(h, wu_ref[...], preferred_element_type=F32)
    act = (gate * _sigmoid(gate)) * up
    x2 = x1 + jnp.dot(act.astype(BF16), wd_ref[...], preferred_element_type=F32)
    o_ref[...] = _rmsnorm(x2, gf_ref[...]) if final else x2


def _ffn(x2d, ya, yb, wo, gn, wg, wu, wd, gf, tm, final):
    t, d = x2d.shape
    wa, wb = ya.shape[-1], yb.shape[-1]
    dff = wg.shape[-1]
    const = lambda m, n: pl.BlockSpec((m, n), lambda i: (0, 0), pipeline_mode=pl.Buffered(1))
    return pl.pallas_call(
        functools.partial(_ffn_kernel, final=final),
        grid=(t // tm,),
        in_specs=[
            pl.BlockSpec((tm, d), lambda i: (i, 0)),
            pl.BlockSpec((tm, wa), lambda i: (i, 0)),
            pl.BlockSpec((tm, wb), lambda i: (i, 0)),
            const(wa + wb, d), const(1, d), const(d, dff), const(d, dff), const(dff, d), const(1, d),
        ],
        out_specs=pl.BlockSpec((tm, d), lambda i: (i, 0)),
        out_shape=jax.ShapeDtypeStruct((t, d), F32),
        compiler_params=pltpu.CompilerParams(
            dimension_semantics=("arbitrary",), vmem_limit_bytes=VMEM_LIMIT_BYTES),
        name="outproj_ffn",
    )(x2d, ya, yb, wo, gn, wg, wu, wd, gf)


def kernel(x, mix_norm_g, w_in, mu_shift, decay_w0, decay_w2, iclr_a0, iclr_a2, gate_g2, k_k, k_a, r_k,
           ln_x_w, ln_x_b, attn_out_g, w_out, ffn_norm_g, w_gate, w_up, w_down, final_norm_g):
    b, s, d = x.shape
    depth = w_in.shape[0]
    w = decay_w0.shape[-1]
    shift_cols = mu_shift.shape[-1]
    attn_w = attn_out_g.shape[-1]
    assert shift_cols == 3 * w + DECAY_LORA + AAA_LORA + GATE_LORA
    assert DECAY_LORA + AAA_LORA == LANES and shift_cols % LANES == 0
    tm = TOKEN_TILE
    rows_per_step = RWKV_ROWS_PER_STEP if b % RWKV_ROWS_PER_STEP == 0 else 1
    x2d = x.reshape(b * s, d)
    for i in range(depth):
        proj = _inproj(x2d, mix_norm_g[i][None], w_in[i], mu_shift[i][None], tm, s).reshape(b, s, -1)
        zeros = jnp.zeros((LANES - DECAY_LORA, w), F32)
        w2p = jnp.concatenate([decay_w2[i], zeros], axis=0)
        a2p = jnp.concatenate([zeros, iclr_a2[i]], axis=0)
        y_a, (wo, wg, wu, wd) = _rwkv(
            proj, decay_w0[i][None], w2p, iclr_a0[i][None], a2p, gate_g2[i],
            k_k[i][None], k_a[i][None], r_k[i].reshape(1, w), ln_x_w[i][None], ln_x_b[i][None],
            nrows=rows_per_step, nchunks=RWKV_CHUNKS_PER_STEP,
            to_bf16=(w_out[i], w_gate[i], w_up[i], w_down[i]))
        y_b = _attention(proj, attn_out_g[i][None], shift_cols, attn_w)
        x2d = _ffn(x2d, y_a.reshape(b * s, w), y_b.reshape(b * s, attn_w), wo, ffn_norm_g[i][None],
                   wg, wu, wd, final_norm_g[None], tm, final=(i == depth - 1))
    return x2d.reshape(b, s, d)
```

```python
import functools

import jax
import jax.numpy as jnp
from jax import lax
from jax.experimental import pallas as pl
from jax.experimental.pallas import tpu as pltpu

HEAD_DIM = 64
LANES = 128
BF16_SUBLANES = 16
DECAY_LORA = 64
AAA_LORA = 64
GATE_LORA = 128
DILATED_PAIRS = ((128, 1), (512, 4), (2048, 16))
ATTN_BLOCK = 128
NORM_EPS = 1e-6
GN_EPS = 64e-5
CHUNK = 64
INVERSE_BASE_BLOCK = 16
KK_NORM_FLOOR = 1e-12
TOKEN_TILE = 512
RWKV_ROWS_PER_STEP = 4
RWKV_CHUNKS_PER_STEP = 2
ATTN_GROUP = 4
ATTN_PERM = 4
DECAY_SCALE = 0.6065306597126334
LOG2_E = 1.4426950408889634
VMEM_LIMIT_BYTES = 56 * 1024 * 1024
F32 = jnp.float32
BF16 = jnp.bfloat16


def _dot(a, b, dims=((1,), (0,))):
    return lax.dot_general(a.astype(BF16), b.astype(BF16), (dims, ((), ())),
                           preferred_element_type=F32)


_NT = ((1,), (1,))
_TN = ((0,), (0,))


def _rmsnorm(x, g):
    return x * lax.rsqrt(jnp.mean(x * x, axis=-1, keepdims=True) + NORM_EPS) * g


def _sigmoid(x):
    return 0.5 * jnp.tanh(0.5 * x) + 0.5


def _inproj_kernel(x_ref, g_ref, w_ref, mu_ref, o_ref, carry_ref, wb_ref, *, tiles_per_seq):
    tm = x_ref.shape[0]
    sc = mu_ref.shape[-1]

    @pl.when(pl.program_id(0) == 0)
    def _():
        for lo in range(0, w_ref.shape[1], LANES * 2):
            wb_ref[:, lo:lo + LANES * 2] = w_ref[:, lo:lo + LANES * 2].astype(BF16)

    @pl.when(pl.program_id(0) % tiles_per_seq == 0)
    def _():
        carry_ref[...] = jnp.zeros_like(carry_ref)

    h = _rmsnorm(x_ref[...], g_ref[...]).astype(BF16)
    p = jnp.dot(h, wb_ref[:, :sc], preferred_element_type=F32)
    first = lax.broadcasted_iota(jnp.int32, (tm, sc), 0) == 0
    prev = jnp.where(first, carry_ref[...], pltpu.roll(p, 1, axis=0))
    carry_ref[...] = p[tm - 1:tm, :]
    o_ref[:, :sc] = p + (prev - p) * mu_ref[...]
    o_ref[:, sc:] = jnp.dot(h, wb_ref[:, sc:], preferred_element_type=F32)


def _inproj(x2d, g, w, mu, tm, seq_len):
    t, d = x2d.shape
    n = w.shape[1]
    sc = mu.shape[-1]
    assert seq_len % tm == 0 and sc % LANES == 0 and n % (2 * LANES) == 0
    return pl.pallas_call(
        functools.partial(_inproj_kernel, tiles_per_seq=seq_len // tm),
        grid=(t // tm,),
        in_specs=[
            pl.BlockSpec((tm, d), lambda i: (i, 0)),
            pl.BlockSpec((1, d), lambda i: (0, 0)),
            pl.BlockSpec((d, n), lambda i: (0, 0), pipeline_mode=pl.Buffered(1)),
            pl.BlockSpec((1, sc), lambda i: (0, 0)),
        ],
        out_specs=pl.BlockSpec((tm, n), lambda i: (i, 0)),
        out_shape=jax.ShapeDtypeStruct((t, n), F32),
        scratch_shapes=[pltpu.VMEM((1, sc), F32), pltpu.VMEM((d, n), BF16)],
        compiler_params=pltpu.CompilerParams(
            dimension_semantics=("arbitrary",), vmem_limit_bytes=VMEM_LIMIT_BYTES),
        name="inproj",
    )(x2d, g, w, mu)


def _unit_lower_inverses(n_mats):
    c = n_mats[0].shape[0]
    ri = lax.broadcasted_iota(jnp.int32, (c, c), 0)
    ci = lax.broadcasted_iota(jnp.int32, (c, c), 1)
    same = lambda size: (ri >> (size.bit_length() - 1)) == (ci >> (size.bit_length() - 1))
    eye = jnp.where(ri == ci, 1.0, 0.0).astype(F32)
    base = INVERSE_BASE_BLOCK
    p = [jnp.where(same(base), n, 0.0) for n in n_mats]
    t = [eye + n for n in p]
    for _ in range(base.bit_length() - 2):
        p = [_dot(x, x) for x in p]
        t = [x + _dot(x, y) for x, y in zip(t, p)]
    size = base
    while size < CHUNK:
        off = [jnp.where(same(2 * size) & jnp.logical_not(same(size)), n, 0.0) for n in n_mats]
        q = [_dot(x, n) for x, n in zip(t, off)]
        t = [x + _dot(y, x) for x, y in zip(t, q)]
        size *= 2
    return t


def _rwkv_kernel(p_ref, w0_ref, w2_ref, a0_ref, a2_ref, g2_ref, kk_ref, ka_ref, rk_ref,
                 lnw_ref, lnb_ref, *rest):
    n_cast = (len(rest) - 2) // 2
    o_ref, state_ref = rest[n_cast], rest[-1]
    for src, dst in zip(rest[:n_cast], rest[n_cast + 1:-1]):
        dst[...] = src[...].astype(BF16)

    c = CHUNK
    nrows = p_ref.shape[0]
    nchunks = p_ref.shape[1] // c
    w = w0_ref.shape[-1]
    n_pairs = w // LANES
    rc = nrows * c
    assert c & (c - 1) == 0 and c >= INVERSE_BASE_BLOCK
    log2_c = c.bit_length() - 1

    @pl.when(pl.program_id(1) == 0)
    def _():
        state_ref[...] = jnp.zeros_like(state_ref)

    first_head = lambda n: lax.broadcasted_iota(jnp.int32, (n, LANES), 1) < HEAD_DIM
    head0 = first_head(c)

    def head_sum(x):
        h0 = first_head(x.shape[0])
        s0 = jnp.sum(jnp.where(h0, x, 0.0), axis=-1, keepdims=True)
        s1 = jnp.sum(jnp.where(h0, 0.0, x), axis=-1, keepdims=True)
        return jnp.where(h0, s0, s1)

    def stack(x):
        return jnp.concatenate([jnp.where(head0, x, 0.0), jnp.where(head0, 0.0, x)], axis=0)

    ti = lax.broadcasted_iota(jnp.int32, (rc, rc), 0)
    si = lax.broadcasted_iota(jnp.int32, (rc, rc), 1)
    tri = jnp.where((si <= ti) & ((si >> log2_c) == (ti >> log2_c)), 1.0, 0.0).astype(BF16)
    t4 = lax.broadcasted_iota(jnp.int32, (4 * c, 4 * c), 0) & (c - 1)
    s4 = lax.broadcasted_iota(jnp.int32, (4 * c, 4 * c), 1) & (c - 1)
    is_r_row = lax.broadcasted_iota(jnp.int32, (4 * c, 4 * c), 0) >= 2 * c
    causal = (s4 < t4) | (is_r_row & (s4 == t4))

    units = [(b, j) for b in range(nrows) for j in range(n_pairs)]
    n_u = range(len(units))
    rows = lambda b: slice(b * c, (b + 1) * c)
    lanes = lambda j: slice(j * LANES, (j + 1) * LANES)

    def operand_stage(ch):
        cols = lambda lo, hi: jnp.concatenate(
            [p_ref[b, ch * c:(ch + 1) * c, lo:hi] for b in range(nrows)], axis=0)
        r = cols(0, w)
        k = cols(w, 2 * w)
        v = cols(2 * w, 3 * w)
        xwa = cols(3 * w, 3 * w + LANES)
        xg = cols(3 * w + LANES, 3 * w + 2 * LANES)

        z = w0_ref[...] + _dot(jnp.tanh(xwa), w2_ref[...])
        ld = -DECAY_SCALE * _sigmoid(z)
        a = _sigmoid(a0_ref[...] + _dot(xwa, a2_ref[...]))
        g = _dot(_sigmoid(xg), g2_ref[...])

        ld_hi = ld.astype(BF16)
        ld_lo = (ld - ld_hi.astype(F32)).astype(BF16)
        l_incl = _dot(tri, ld_hi) + _dot(tri, ld_lo)
        e_incl = jnp.exp(l_incl)
        e_excl = jnp.exp(l_incl - ld)
        e_inv = jnp.exp(-l_incl)

        kk_all = k * kk_ref[...]
        k2_all = k * (1.0 + (a - 1.0) * ka_ref[...])
        rk_all = r * k2_all * rk_ref[...]
        at_all, bt_all = [], []
        for j in range(n_pairs):
            sl = lanes(j)
            kk = kk_all[:, sl]
            kk = kk * lax.rsqrt(jnp.maximum(head_sum(kk * kk), KK_NORM_FLOOR ** 2))
            at_all.append(-kk * e_excl[:, sl])
            bt_all.append(kk * a[:, sl] * e_inv[:, sl])
        rt_all = r * e_incl
        kt_all = k2_all * e_inv

        ar_s = [jnp.concatenate([stack(at_all[j][rows(b)]), stack(rt_all[rows(b), lanes(j)])],
                                axis=0).astype(BF16) for b, j in units]
        bk_s = [jnp.concatenate([stack(bt_all[j][rows(b)]), stack(kt_all[rows(b), lanes(j)])],
                                axis=0) for b, j in units]
        v_s = [stack(v[rows(b), lanes(j)]) for b, j in units]
        wc = [e_incl[b * c + c - 1:(b + 1) * c, lanes(j)] for b, j in units]
        bonus = [head_sum(rk_all[rows(b), lanes(j)]) * v[rows(b), lanes(j)] for b, j in units]
        gate = [g[rows(b), lanes(j)] for b, j in units]
        return ar_s, bk_s, v_s, wc, bonus, gate

    def state_free_stage(operands):
        ar_s, bk_s, v_s, wc, bonus, gate = operands
        m = [jnp.where(causal, _dot(ar_s[i], bk_s[i], _NT), 0.0) for i in n_u]
        t_inv = _unit_lower_inverses([m[i][:2 * c, :2 * c] for i in n_u])
        akv = [_dot(m[i][:2 * c, 2 * c:], v_s[i]) for i in n_u]
        m_r = [m[i][2 * c:, :] for i in n_u]
        return ar_s, bk_s, v_s, wc, t_inv, akv, m_r, bonus, gate

    def state_stage(ch, hs, ops):
        ar_s, bk_s, v_s, wc, t_inv, akv, m_r, bonus, gate = ops
        arh = [_dot(ar_s[i], hs[i], _NT) for i in n_u]
        u = [_dot(t_inv[i], arh[i][:2 * c] + akv[i]) for i in n_u]
        uv = [jnp.concatenate([u[i], v_s[i]], axis=0) for i in n_u]
        hs_new = [(hs[i] + _dot(uv[i], bk_s[i], _TN)) * wc[i] for i in n_u]
        y_s = [arh[i][2 * c:] + _dot(m_r[i], uv[i]) for i in n_u]
        for i, (b, j) in enumerate(units):
            sl = lanes(j)
            y = y_s[i][:c] + y_s[i][c:]
            mean = head_sum(y) * (1.0 / HEAD_DIM)
            yc = y - mean
            var = head_sum(yc * yc) * (1.0 / HEAD_DIM)
            yn = yc * lax.rsqrt(var + GN_EPS) * lnw_ref[:, sl] + lnb_ref[:, sl]
            o_ref[b, ch * c:(ch + 1) * c, sl] = (yn + bonus[i]) * gate[i]
        return hs_new

    operands = [operand_stage(ch) for ch in range(nchunks)]
    ops = [state_free_stage(x) for x in operands]
    hs = [state_ref[i] for i in n_u]
    for ch in range(nchunks):
        hs = state_stage(ch, hs, ops[ch])
    for i in n_u:
        state_ref[i] = hs[i]


def _rwkv(proj, w0, w2p, a0, a2p, g2, k_k, k_a, r_k, ln_w, ln_b, nrows, nchunks, to_bf16):
    b, s, _ = proj.shape
    w = w0.shape[-1]
    shift_cols = 3 * w + 2 * LANES
    c = CHUNK * nchunks
    grid = (b // nrows, s // c)
    steps = grid[0] * grid[1]
    row = lambda n: pl.BlockSpec((1, n), lambda bi, ci: (0, 0))
    mat = lambda m, n: pl.BlockSpec((m, n), lambda bi, ci: (0, 0))

    def cast_spec(x):
        parts = steps
        while x.shape[0] % (parts * BF16_SUBLANES):
            parts //= 2
        assert parts >= 1 and steps % parts == 0
        return pl.BlockSpec((x.shape[0] // parts, x.shape[1]),
                            lambda bi, ci: (jnp.minimum(bi * grid[1] + ci, parts - 1), 0))

    cast_specs = [cast_spec(x) for x in to_bf16]
    out = pl.pallas_call(
        _rwkv_kernel,
        grid=grid,
        in_specs=[
            pl.BlockSpec((nrows, c, shift_cols), lambda bi, ci: (bi, ci, 0)),
            row(w), mat(LANES, w), row(w), mat(LANES, w), mat(GATE_LORA, w),
            row(w), row(w), row(w), row(w), row(w),
        ] + cast_specs,
        out_specs=[pl.BlockSpec((nrows, c, w), lambda bi, ci: (bi, ci, 0))] + cast_specs,
        out_shape=[jax.ShapeDtypeStruct((b, s, w), F32)]
                  + [jax.ShapeDtypeStruct(x.shape, BF16) for x in to_bf16],
        scratch_shapes=[pltpu.VMEM((nrows * (w // LANES), LANES, LANES), F32)],
        compiler_params=pltpu.CompilerParams(
            dimension_semantics=("arbitrary", "arbitrary"), vmem_limit_bytes=VMEM_LIMIT_BYTES),
        name="rwkv7",
    )(proj, w0, w2p, a0, a2p, g2, k_k, k_a, r_k, ln_w, ln_b, *to_bf16)
    return out[0], out[1:]


def _attn_kernel(*refs):
    PERM = ATTN_PERM
    cls = [refs[i * PERM:(i + 1) * PERM] for i in range(3)]
    k_ref, v_ref, g_ref, o_ref, ob_ref, mx_ref, den_ref = refs[3 * PERM:]
    s_len = k_ref.shape[0]
    blk = ATTN_BLOCK
    scale = HEAD_DIM ** -0.5
    lane = lax.broadcasted_iota(jnp.int32, (blk, LANES), 1)
    head0 = lane < HEAD_DIM

    quarter = s_len // PERM
    run = blk // PERM
    assert run % 8 == 0 and all(d == 1 or d % PERM == 0 for _, d in DILATED_PAIRS) and s_len % PERM == 0

    def stack(x):
        return jnp.concatenate([jnp.where(head0, x, 0.0), jnp.where(head0, 0.0, x)], axis=0)

    def segments(s):
        return [pl.ds(b * quarter + s // PERM, run) for b in range(PERM)]

    def window_bias(span, has_prev, perm_q):
        nk = 2 * blk if has_prev else blk
        qi = lax.broadcasted_iota(jnp.int32, (2 * blk, nk), 0) & (blk - 1)
        if perm_q:
            qi = (qi & (run - 1)) * PERM + qi // run
        kj = lax.broadcasted_iota(jnp.int32, (2 * blk, nk), 1)
        rel = qi - kj + (blk if has_prev else 0)
        return jnp.where((rel >= 0) & (rel <= span), 0.0, -jnp.inf).astype(F32)

    def scores_stage(bi, dil, bias, blocks):
        if dil > 1:
            def ds(s):
                off = (s % PERM) * quarter + s // PERM
                return pl.ds(off, blk, stride=dil // PERM) if dil > PERM else pl.ds(off, blk)
            inner = lambda s: (pl.ds(s // PERM, blk, stride=dil // PERM) if dil > PERM
                               else pl.ds(s // PERM, blk))
            rd = lambda i, ref, s: cls[i][s % PERM][inner(s), :]
            rows = [ds(s) for s, _ in blocks]
            q_in = [rd(0, None, s) for s, _ in blocks]
        else:
            rd = lambda i, ref, s: ref[pl.ds(s, blk), :]
            rows = [s for s, _ in blocks]
            q_in = [jnp.concatenate([cls[0][b][pl.ds(s // PERM, run), :] for b in range(PERM)], axis=0)
                    for s in rows]
        q_s = [stack(x * (scale * LOG2_E)).astype(BF16) for x in q_in]
        kcat = [rd(1, k_ref, s).astype(BF16) if ps is None else
                jnp.concatenate([rd(1, k_ref, ps), rd(1, k_ref, s)], axis=0).astype(BF16)
                for s, ps in blocks]
        vcat = [rd(2, v_ref, s).astype(BF16) if ps is None else
                jnp.concatenate([rd(2, v_ref, ps), rd(2, v_ref, s)], axis=0).astype(BF16)
                for s, ps in blocks]
        sc = [_dot(q, kc, _NT) + bias[ps is not None]
              for q, kc, (_, ps) in zip(q_s, kcat, blocks)]
        return bi, rows, sc, vcat

    def softmax_stage(bi, rows, sc, vcat):
        mx = [jnp.max(x, axis=-1, keepdims=True) for x in sc]
        pe = [jnp.exp2(x - m).astype(BF16) for x, m in zip(sc, mx)]
        return bi, rows, pe, mx, vcat

    def pv_stage(pe, mx, vcat):
        ones = jnp.ones((2 * blk, LANES), BF16)
        ov = [_dot(x, jnp.concatenate([vc, ones[:vc.shape[0]]], axis=1)) for x, vc in zip(pe, vcat)]
        pair = lambda x: jnp.where(head0, x[:blk], x[blk:])
        return [(pair(x[:, :LANES]), pair(m), pair(x[:, LANES:])) for x, m in zip(ov, mx)]

    add = lambda xs: functools.reduce(lambda x, y: x + y, xs)

    def store_stage(slot, rows, res):
        for r, (o, m, d) in zip(rows, res):
            ob_ref[slot, r, :] = o
            mx_ref[slot, r, :] = m
            den_ref[slot, r, :] = d

    def mix_stage(rows, res):
        slots = range(ob_ref.shape[0])
        for start, (o_here, m_here, d_here) in zip(rows, res):
            segs = segments(start)
            ld = lambda ref, s: jnp.concatenate([ref[s, sg, :] for sg in segs], axis=0)
            ms_ = [m_here] + [ld(mx_ref, s) for s in slots]
            os_ = [o_here] + [ld(ob_ref, s) for s in slots]
            ds_ = [d_here] + [ld(den_ref, s) for s in slots]
            top = functools.reduce(jnp.maximum, ms_)
            es = [jnp.exp2(m - top) for m in ms_]
            o = add([e * x for e, x in zip(es, os_)]) / add([e * d for e, d in zip(es, ds_)])
            o2 = o * o
            s0 = jnp.sum(jnp.where(head0, o2, 0.0), axis=-1, keepdims=True)
            s1 = jnp.sum(jnp.where(head0, 0.0, o2), axis=-1, keepdims=True)
            ms = jnp.where(head0, s0, s1) * (1.0 / HEAD_DIM)
            out = o * lax.rsqrt(ms + NORM_EPS) * g_ref[...]
            for b in range(PERM):
                o_ref[pl.ds(start + b, run, stride=PERM), :] = out[b * run:(b + 1) * run]

    group = ATTN_GROUP
    groups = []
    order = sorted(range(len(DILATED_PAIRS)), key=lambda b: -DILATED_PAIRS[b][1])
    assert DILATED_PAIRS[order[-1]][1] == 1
    for bi in order:
        window, dil = DILATED_PAIRS[bi]
        sub_len = s_len // dil
        span = window // dil
        assert sub_len % blk == 0
        nb = sub_len // blk
        blocks = [(n * blk * dil + r, (n - 1) * blk * dil + r if n else None)
                  for r in range(dil) for n in range(nb)]
        assert len(blocks) % group == 0
        groups += [(bi, dil, span, nb, blocks[i:i + group]) for i in range(0, len(blocks), group)]

    def finish(bi, rows, pe, mx, vcat):
        res = pv_stage(pe, mx, vcat)
        if bi == order[-1]:
            mix_stage(rows, res)
        else:
            store_stage(order.index(bi), rows, res)

    biases = {}
    scored, soft = None, None
    for bi, dil, span, nb, blocks in groups + [(None,) * 5] * 2:
        nxt = None
        if blocks is not None:
            key = (span, nb, dil == 1)
            if key not in biases:
                biases[key] = {False: window_bias(span, False, dil == 1),
                               True: window_bias(span, True, dil == 1) if nb > 1 else None}
            nxt = scores_stage(bi, dil, biases[key], blocks)
        nxt_soft = softmax_stage(*scored) if scored is not None else None
        if soft is not None:
            finish(*soft)
        scored, soft = nxt, nxt_soft


def _attention(proj, out_g, q_col0, width):
    b, s, _ = proj.shape
    n_pairs = width // LANES
    qb = q_col0 // LANES
    spec = lambda off: pl.BlockSpec((None, s, LANES), lambda bi, pi: (bi, 0, off + pi))
    ncb = proj.shape[-1] // LANES
    assert proj.shape[-1] % LANES == 0 and s % ATTN_PERM == 0
    proj_c = proj.reshape(b, s // ATTN_PERM, ATTN_PERM * proj.shape[-1])
    cspec = lambda off, c: pl.BlockSpec((None, s // ATTN_PERM, LANES),
                                        lambda bi, pi: (bi, 0, c * ncb + off + pi))
    classes = [cspec(qb + i * n_pairs, c) for i in range(3) for c in range(ATTN_PERM)]
    return pl.pallas_call(
        _attn_kernel,
        grid=(b, n_pairs),
        in_specs=classes + [spec(qb + n_pairs), spec(qb + 2 * n_pairs),
                            pl.BlockSpec((1, LANES), lambda bi, pi: (0, pi))],
        out_specs=pl.BlockSpec((None, s, LANES), lambda bi, pi: (bi, 0, pi)),
        out_shape=jax.ShapeDtypeStruct((b, s, width), F32),
        scratch_shapes=[
            pltpu.VMEM((len(DILATED_PAIRS) - 1, s, LANES), F32) for _ in range(3)],
        compiler_params=pltpu.CompilerParams(
            dimension_semantics=("arbitrary", "arbitrary"), vmem_limit_bytes=VMEM_LIMIT_BYTES),
        name="dilated_attn",
    )(*([proj_c] * (3 * ATTN_PERM)), proj, proj, out_g)


def _ffn_kernel(x_ref, ya_ref, yb_ref, wo_ref, gn_ref, wg_ref, wu_ref, wd_ref, gf_ref, o_ref, *, final):
    wa = ya_ref.shape[-1]
    x1 = (x_ref[...]
          + jnp.dot(ya_ref[...].astype(BF16), wo_ref[0:wa, :], preferred_element_type=F32)
          + jnp.dot(yb_ref[...].astype(BF16), wo_ref[wa:, :], preferred_element_type=F32))
    h = _rmsnorm(x1, gn_ref[...]).astype(BF16)
    gate = jnp.dot(h, wg_ref[...], preferred_element_type=F32)
    up = jnp.dot(h, wu_ref[...], preferred_element_type=F32)
    act = (gate * _sigmoid(gate)) * up
    x2 = x1 + jnp.dot(act.astype(BF16), wd_ref[...], preferred_element_type=F32)
    o_ref[...] = _rmsnorm(x2, gf_ref[...]) if final else x2


def _ffn(x2d, ya, yb, wo, gn, wg, wu, wd, gf, tm, final):
    t, d = x2d.shape
    wa, wb = ya.shape[-1], yb.shape[-1]
    dff = wg.shape[-1]
    const = lambda m, n: pl.BlockSpec((m, n), lambda i: (0, 0), pipeline_mode=pl.Buffered(1))
    return pl.pallas_call(
        functools.partial(_ffn_kernel, final=final),
        grid=(t // tm,),
        in_specs=[
            pl.BlockSpec((tm, d), lambda i: (i, 0)),
            pl.BlockSpec((tm, wa), lambda i: (i, 0)),
            pl.BlockSpec((tm, wb), lambda i: (i, 0)),
            const(wa + wb, d), const(1, d), const(d, dff), const(d, dff), const(dff, d), const(1, d),
        ],
        out_specs=pl.BlockSpec((tm, d), lambda i: (i, 0)),
        out_shape=jax.ShapeDtypeStruct((t, d), F32),
        compiler_params=pltpu.CompilerParams(
            dimension_semantics=("arbitrary",), vmem_limit_bytes=VMEM_LIMIT_BYTES),
        name="outproj_ffn",
    )(x2d, ya, yb, wo, gn, wg, wu, wd, gf)


def kernel(x, mix_norm_g, w_in, mu_shift, decay_w0, decay_w2, iclr_a0, iclr_a2, gate_g2, k_k, k_a, r_k,
           ln_x_w, ln_x_b, attn_out_g, w_out, ffn_norm_g, w_gate, w_up, w_down, final_norm_g):
    b, s, d = x.shape
    depth = w_in.shape[0]
    w = decay_w0.shape[-1]
    shift_cols = mu_shift.shape[-1]
    attn_w = attn_out_g.shape[-1]
    assert shift_cols == 3 * w + DECAY_LORA + AAA_LORA + GATE_LORA
    assert DECAY_LORA + AAA_LORA == LANES and shift_cols % LANES == 0
    tm = TOKEN_TILE
    rows_per_step = RWKV_ROWS_PER_STEP if b % RWKV_ROWS_PER_STEP == 0 else 1
    x2d = x.reshape(b * s, d)
    for i in range(depth):
        proj = _inproj(x2d, mix_norm_g[i][None], w_in[i], mu_shift[i][None], tm, s).reshape(b, s, -1)
        zeros = jnp.zeros((LANES - DECAY_LORA, w), F32)
        w2p = jnp.concatenate([decay_w2[i], zeros], axis=0)
        a2p = jnp.concatenate([zeros, iclr_a2[i]], axis=0)
        y_a, (wo, wg, wu, wd) = _rwkv(
            proj, decay_w0[i][None], w2p, iclr_a0[i][None], a2p, gate_g2[i],
            k_k[i][None], k_a[i][None], r_k[i].reshape(1, w), ln_x_w[i][None], ln_x_b[i][None],
            nrows=rows_per_step, nchunks=RWKV_CHUNKS_PER_STEP,
            to_bf16=(w_out[i], w_gate[i], w_up[i], w_down[i]))
        y_b = _attention(proj, attn_out_g[i][None], shift_cols, attn_w)
        x2d = _ffn(x2d, y_a.reshape(b * s, w), y_b.reshape(b * s, attn_w), wo, ffn_norm_g[i][None],
                   wg, wu, wd, final_norm_g[None], tm, final=(i == depth - 1))
    return x2d.reshape(b, s, d)
```
